```python
import math
import jax, jax.numpy as jnp
from jax import lax
import numpy as np

D_MODEL = 1024
BATCH = 4
SEQ = 4096
DEPTH = 2

CHUNK = 64
D_CONV = D_MODEL // 4
CONV_WIDTH = 31
REC_DK = 128
REC_DV = 128
D_REC = D_MODEL // 2
REC_HEADS = D_REC // REC_DV
ATT_HEAD_DIM = 64
D_ATT = D_MODEL // 4
ATT_HEADS = D_ATT // ATT_HEAD_DIM
ATT_LEFT_CHUNKS = 8
MAX_REL = 128
D_MIX = D_CONV + D_REC + D_ATT
D_IN = 2 * D_CONV + 4 * D_REC + 3 * D_ATT
D_FF = ((8 * D_MODEL // 3 + 255) // 256) * 256
ALPHA = (2 * DEPTH) ** 0.25
BETA = (8 * DEPTH) ** -0.25
LN_EPS = 1e-5
NEG_BIG = -1e30
TINY = 1e-30

kernel_name = 'hybrid_conformer_hgrn2_chunkattn_deepnorm'


def layer_norm(x, g, b):
    xf = x.astype(jnp.float32)
    mu = jnp.mean(xf, axis=-1, keepdims=True)
    var = jnp.mean(jnp.square(xf - mu), axis=-1, keepdims=True)
    y = (xf - mu) * lax.rsqrt(var + LN_EPS) * g.astype(jnp.float32) + b.astype(jnp.float32)
    return y.astype(x.dtype)


def conv_module(a_val, a_gate, w, bias, ln_g, ln_b):
    u = a_val * jax.nn.sigmoid(a_gate)
    u = lax.conv_general_dilated(
        u, w[:, None, :].astype(u.dtype), window_strides=(1,),
        padding=[(CONV_WIDTH - 1, 0)],
        dimension_numbers=('NWC', 'WIO', 'NWC'),
        feature_group_count=D_CONV) + bias
    return jax.nn.silu(layer_norm(u, ln_g, ln_b))


def hgrn2_mixer(q, f_logit, i, g, lb, norm_g):
    B, T, _ = q.shape
    nc = T // CHUNK
    f32 = jnp.float32
    zf = f_logit.astype(f32)
    lb = lb.astype(f32)
    log_f = jnp.logaddexp(jnp.log(jnp.maximum(lb, TINY)),
                          jnp.log1p(-lb) + jax.nn.log_sigmoid(zf))
    k = (1.0 - lb) * jax.nn.sigmoid(-zf)

    def to_chunks(t, d):
        return t.astype(f32).reshape(B, nc, CHUNK, REC_HEADS, d).transpose(1, 0, 3, 2, 4)

    qc, kc, lfc = to_chunks(q, REC_DK), to_chunks(k, REC_DK), to_chunks(log_f, REC_DK)
    vc = to_chunks(i, REC_DV)
    causal = jnp.tril(jnp.ones((CHUNK, CHUNK), dtype=bool))

    def step(S, inp):
        qq, kk, vv, lf = inp
        bcum = jnp.cumsum(lf, axis=2)
        o_inter = jnp.einsum('bhtd,bhde->bhte', qq * jnp.exp(bcum), S)
        diff = bcum[:, :, :, None, :] - bcum[:, :, None, :, :]
        decay = jnp.exp(jnp.where(causal[:, :, None], diff, NEG_BIG))
        attn = jnp.einsum('bhtsd,bhsd->bhts', qq[:, :, :, None, :] * decay, kk)
        o_intra = jnp.einsum('bhts,bhse->bhte', attn, vv)
        b_last = bcum[:, :, -1:, :]
        S_new = jnp.exp(b_last[:, :, 0, :])[..., None] * S + jnp.einsum(
            'bhsd,bhse->bhde', kk * jnp.exp(b_last - bcum), vv)
        return S_new, o_inter + o_intra

    S0 = jnp.zeros((B, REC_HEADS, REC_DK, REC_DV), f32)
    _, o = lax.scan(step, S0, (qc, kc, vc, lfc))
    o = o.transpose(1, 0, 3, 2, 4).reshape(B, T, REC_HEADS, REC_DV)
    o = o * lax.rsqrt(jnp.mean(jnp.square(o), axis=-1, keepdims=True) + LN_EPS) * norm_g.astype(f32)
    o = o.reshape(B, T, D_REC) * jax.nn.silu(g.astype(f32))
    return o.astype(q.dtype)


def chunk_attention(q, k, v, rel_table):
    B, T, _ = q.shape
    nc = T // CHUNK
    W = ATT_LEFT_CHUNKS + 1
    f32 = jnp.float32
    shp = (B, nc, CHUNK, ATT_HEADS, ATT_HEAD_DIM)
    qc = q.astype(f32).reshape(shp)
    pad = ((0, 0), (ATT_LEFT_CHUNKS, 0), (0, 0), (0, 0), (0, 0))
    kp = jnp.pad(k.astype(f32).reshape(shp), pad)
    vp = jnp.pad(v.astype(f32).reshape(shp), pad)
    kb = jnp.concatenate([kp[:, j:j + nc] for j in range(W)], axis=2)
    vb = jnp.concatenate([vp[:, j:j + nc] for j in range(W)], axis=2)
    s = jnp.einsum('bnqhd,bnkhd->bhnqk', qc, kb) * (ATT_HEAD_DIM ** -0.5)
    rel = (jnp.arange(W * CHUNK)[None, :] - ATT_LEFT_CHUNKS * CHUNK
           - jnp.arange(CHUNK)[:, None])
    idx = jnp.clip(rel, -MAX_REL, MAX_REL) + MAX_REL
    bias = rel_table.astype(f32)[:, idx]
    key_chunk = (jnp.arange(nc)[:, None] + jnp.repeat(jnp.arange(W), CHUNK)[None, :]
                 - ATT_LEFT_CHUNKS)
    valid = key_chunk >= 0
    s = s + bias[:, None]
    s = jnp.where(valid[:, None, :], s, NEG_BIG)
    p = jax.nn.softmax(s, axis=-1)
    o = jnp.einsum('bhnqk,bnkhd->bnqhd', p, vb).reshape(B, T, D_ATT)
    return o.astype(q.dtype)


def setup_inputs(seed: int = 0) -> dict:
    key = jax.random.key(seed)
    ks = jax.random.split(key, 24)
    f32 = jnp.float32
    nrm = lambda k, shape, s: (jax.random.normal(k, shape, f32) * s)
    x = jax.random.normal(ks[0], (BATCH, SEQ, D_MODEL), f32)
    c = jax.random.normal(ks[1], (BATCH, D_MODEL), f32)
    w_ada = nrm(ks[2], (DEPTH, D_MODEL, 6 * D_MODEL), 0.1 * D_MODEL ** -0.5)
    b_ada = nrm(ks[3], (DEPTH, 6 * D_MODEL), 0.01)
    s_in = D_MODEL ** -0.5
    w_in = jnp.concatenate([
        nrm(ks[4], (DEPTH, D_MODEL, D_CONV), s_in * BETA),
        nrm(ks[5], (DEPTH, D_MODEL, D_CONV), s_in),
        nrm(ks[6], (DEPTH, D_MODEL, D_REC), s_in),
        nrm(ks[7], (DEPTH, D_MODEL, D_REC), s_in),
        nrm(ks[8], (DEPTH, D_MODEL, D_REC), s_in * BETA),
        nrm(ks[9], (DEPTH, D_MODEL, D_REC), s_in),
        nrm(ks[10], (DEPTH, D_MODEL, D_ATT), s_in),
        nrm(ks[11], (DEPTH, D_MODEL, D_ATT), s_in),
        nrm(ks[12], (DEPTH, D_MODEL, D_ATT), s_in * BETA)], axis=-1)
    conv_w = nrm(ks[13], (DEPTH, CONV_WIDTH, D_CONV), CONV_WIDTH ** -0.5)
    conv_b = nrm(ks[14], (DEPTH, D_CONV), 0.01)
    conv_ln_g = 1.0 + nrm(ks[15], (DEPTH, D_CONV), 0.01)
    conv_ln_b = nrm(ks[16], (DEPTH, D_CONV), 0.01)
    rec_lower_bound = 1.0 + nrm(ks[17], (DEPTH, D_REC), 0.1)
    rec_norm_g = 1.0 + nrm(ks[18], (DEPTH, REC_DV), 0.01)
    rel_bias = nrm(ks[19], (DEPTH, ATT_HEADS, 2 * MAX_REL + 1), 0.1)
    w_out = nrm(ks[20], (DEPTH, D_MIX, D_MODEL), D_MIX ** -0.5 * BETA)
    kln = jax.random.split(ks[21], 4)
    ln1_g = 1.0 + nrm(kln[0], (DEPTH, D_MODEL), 0.01)
    ln1_b = nrm(kln[1], (DEPTH, D_MODEL), 0.01)
    ln2_g = 1.0 + nrm(kln[2], (DEPTH, D_MODEL), 0.01)
    ln2_b = nrm(kln[3], (DEPTH, D_MODEL), 0.01)
    w_ffn_in = nrm(ks[22], (DEPTH, D_MODEL, 2 * D_FF), s_in * BETA)
    w_ffn_out = nrm(ks[23], (DEPTH, D_FF, D_MODEL), D_FF ** -0.5 * BETA)
    return {'x': x, 'c': c, 'w_ada': w_ada, 'b_ada': b_ada, 'w_in': w_in,
            'conv_w': conv_w, 'conv_b': conv_b, 'conv_ln_g': conv_ln_g, 'conv_ln_b': conv_ln_b,
            'rec_lower_bound': rec_lower_bound, 'rec_norm_g': rec_norm_g, 'rel_bias': rel_bias,
            'w_out': w_out, 'ln1_g': ln1_g, 'ln1_b': ln1_b, 'ln2_g': ln2_g, 'ln2_b': ln2_b,
            'w_ffn_in': w_ffn_in, 'w_ffn_out': w_ffn_out}


def reference(x, c, w_ada, b_ada, w_in, conv_w, conv_b, conv_ln_g, conv_ln_b,
              rec_lower_bound, rec_norm_g, rel_bias, w_out, ln1_g, ln1_b, ln2_g, ln2_b,
              w_ffn_in, w_ffn_out):
    lbs = jax.nn.softmax(rec_lower_bound.astype(jnp.float32), axis=0)
    lbs = jnp.cumsum(lbs, axis=0) - lbs[0]
    split_pts = [D_CONV, 2 * D_CONV,
                 2 * D_CONV + D_REC, 2 * D_CONV + 2 * D_REC,
                 2 * D_CONV + 3 * D_REC, 2 * D_CONV + 4 * D_REC,
                 2 * D_CONV + 4 * D_REC + D_ATT, 2 * D_CONV + 4 * D_REC + 2 * D_ATT]
    c_act = jax.nn.silu(c)
    for l in range(DEPTH):
        mod = (c_act @ w_ada[l] + b_ada[l])[:, None, :]
        sh1, sc1, g1, sh2, sc2, g2 = jnp.split(mod, 6, axis=-1)
        h = x * (1.0 + sc1) + sh1
        p = h @ w_in[l]
        a_val, a_gate, rq, rf, ri, rg, aq, ak, av = jnp.split(p, split_pts, axis=-1)
        y_conv = conv_module(a_val, a_gate, conv_w[l], conv_b[l], conv_ln_g[l], conv_ln_b[l])
        y_rec = hgrn2_mixer(rq, rf, ri, rg, lbs[l], rec_norm_g[l])
        y_att = chunk_attention(aq, ak, av, rel_bias[l])
        y = jnp.concatenate([y_conv, y_rec, y_att], axis=-1) @ w_out[l]
        x = layer_norm(ALPHA * x + (1.0 + g1) * y, ln1_g[l], ln1_b[l])
        h = x * (1.0 + sc2) + sh2
        gt, up = jnp.split(h @ w_ffn_in[l], 2, axis=-1)
        y = (jax.nn.silu(gt) * up) @ w_ffn_out[l]
        x = layer_norm(ALPHA * x + (1.0 + g2) * y, ln2_g[l], ln2_b[l])
    return x
```

```python
import functools
import math

import numpy as np
import jax
import jax.numpy as jnp
from jax import lax
from jax.experimental import pallas as pl
from jax.experimental.pallas import tpu as pltpu

F32 = jnp.float32
BF16 = jnp.bfloat16

CHUNK = 64
CONV_WIDTH = 31
REC_HEADS = 4
REC_D = 128
ATT_HEADS = 4
ATT_DH = 64
ATT_LEFT = 8
MAX_REL = 128
LN_EPS = 1e-5
NEG_BIG = -1e30
TINY = 1e-30

SUB = 16
ATT_QB = 256
ATT_KB = ATT_QB + ATT_LEFT * CHUNK
CONV_TAIL = 32
CONV_RS = 64

VMEM_LIMIT = 56 * 1024 * 1024


def _cparams(sem):
    return pltpu.CompilerParams(dimension_semantics=sem, vmem_limit_bytes=VMEM_LIMIT)


def _layer_norm(x, g, b):
    mu = jnp.mean(x, axis=-1, keepdims=True)
    xc = x - mu
    var = jnp.mean(xc * xc, axis=-1, keepdims=True)
    return xc * lax.rsqrt(var + LN_EPS) * g + b


def _silu(x):
    return x * jax.nn.sigmoid(x)


def _ada_kernel(c_ref, w_ref, b_ref, o_ref):
    ca = _silu(c_ref[...])
    o_ref[0] = jnp.dot(ca, w_ref[0], precision=lax.Precision.HIGHEST,
                       preferred_element_type=F32) + b_ref[0]


def _ada_mod(c, w_ada, b_ada):
    depth, d, n6 = w_ada.shape
    b = c.shape[0]
    bp = ((b + 7) // 8) * 8
    cp = jnp.zeros((bp, d), F32).at[:b].set(c)
    tn = 1536
    out = pl.pallas_call(
        _ada_kernel,
        out_shape=jax.ShapeDtypeStruct((depth, bp, n6), F32),
        grid=(depth, n6 // tn),
        in_specs=[pl.BlockSpec((bp, d), lambda l, j: (0, 0)),
                  pl.BlockSpec((1, d, tn), lambda l, j: (l, 0, j)),
                  pl.BlockSpec((1, 1, tn), lambda l, j: (l, 0, j))],
        out_specs=pl.BlockSpec((1, bp, tn), lambda l, j: (l, 0, j)),
        compiler_params=_cparams(("arbitrary", "arbitrary")),
        name="ada_mod",
    )(cp, w_ada, b_ada.reshape(depth, 1, n6))
    return out[:, :b]


def _inproj_kernel(x_ref, sc_ref, sh_ref, w_ref, o_ref):
    h = x_ref[0] * (1.0 + sc_ref[0]) + sh_ref[0]
    o_ref[0] = jnp.dot(h.astype(BF16), w_ref[...], preferred_element_type=F32)


def _in_proj(x, sc, sh, w_bf16, tm):
    b, t, d = x.shape
    n = w_bf16.shape[1]
    return pl.pallas_call(
        _inproj_kernel,
        out_shape=jax.ShapeDtypeStruct((b, t, n), F32),
        grid=(b, t // tm),
        in_specs=[pl.BlockSpec((1, tm, d), lambda i, j: (i, j, 0)),
                  pl.BlockSpec((1, 1, d), lambda i, j: (i, 0, 0)),
                  pl.BlockSpec((1, 1, d), lambda i, j: (i, 0, 0)),
                  pl.BlockSpec((d, n), lambda i, j: (0, 0))],
        out_specs=pl.BlockSpec((1, tm, n), lambda i, j: (i, j, 0)),
        compiler_params=_cparams(("arbitrary", "arbitrary")),
        name="in_proj",
    )(x, sc, sh, w_bf16)


def _conv_kernel(av_ref, ag_ref, w_ref, b_ref, g_ref, be_ref, o_ref, ubuf):
    tb = av_ref.shape[1]
    t = pl.program_id(1)

    @pl.when(t == 0)
    def _():
        ubuf[0:CONV_TAIL, :] = jnp.zeros((CONV_TAIL, ubuf.shape[1]), F32)

    @pl.when(t > 0)
    def _():
        ubuf[0:CONV_TAIL, :] = ubuf[tb:tb + CONV_TAIL, :]

    ubuf[CONV_TAIL:CONV_TAIL + tb, :] = av_ref[0] * jax.nn.sigmoid(ag_ref[0])
    off = CONV_TAIL - (CONV_WIDTH - 1)
    for r in range(tb // CONV_RS):
        acc = jnp.zeros((CONV_RS, ubuf.shape[1]), F32)
        for j in range(CONV_WIDTH):
            s = r * CONV_RS + off + j
            acc = acc + w_ref[j:j + 1, :] * ubuf[s:s + CONV_RS, :]
        y = _layer_norm(acc + b_ref[...], g_ref[...], be_ref[...])
        o_ref[0, r * CONV_RS:(r + 1) * CONV_RS, :] = _silu(y).astype(o_ref.dtype)


def _conv_mixer(p, w, bias, ln_g, ln_b, tb):
    b, t, _ = p.shape
    c = w.shape[1]
    row = lambda a: a.reshape(1, c)
    return pl.pallas_call(
        _conv_kernel,
        out_shape=jax.ShapeDtypeStruct((b, t, c), BF16),
        grid=(b, t // tb),
        in_specs=[pl.BlockSpec((1, tb, c), lambda i, j: (i, j, 0)),
                  pl.BlockSpec((1, tb, c), lambda i, j: (i, j, 1)),
                  pl.BlockSpec((CONV_WIDTH, c), lambda i, j: (0, 0)),
                  pl.BlockSpec((1, c), lambda i, j: (0, 0)),
                  pl.BlockSpec((1, c), lambda i, j: (0, 0)),
                  pl.BlockSpec((1, c), lambda i, j: (0, 0))],
        out_specs=pl.BlockSpec((1, tb, c), lambda i, j: (i, j, 0)),
        scratch_shapes=[pltpu.VMEM((tb + CONV_TAIL, c), F32)],
        compiler_params=_cparams(("arbitrary", "arbitrary")),
        name="conv_mixer",
    )(p, p, w, row(bias), row(ln_g), row(ln_b))


def _split3(x):
    hi = x.astype(BF16)
    r1 = x - hi.astype(F32)
    mid = r1.astype(BF16)
    lo = (r1 - mid.astype(F32)).astype(BF16)
    return hi, mid, lo


def _rec_kernel(q_ref, f_ref, i_ref, g_ref, lb_ref, ng_ref, o_ref,
                st_ref, bc_s, k_s, v_s, oi_s):
    tb = q_ref.shape[1]
    dd = q_ref.shape[2]
    nh = dd // REC_D
    t = pl.program_id(1)

    @pl.when(t == 0)
    def _():
        st_ref[...] = jnp.zeros(st_ref.shape, F32)

    lb = lb_ref[...]
    log_lb = jnp.log(jnp.maximum(lb, TINY))
    log1m_lb = jnp.log1p(-lb)
    one_m_lb = 1.0 - lb
    rows = lax.broadcasted_iota(jnp.int32, (CHUNK, CHUNK), 0)
    cols = lax.broadcasted_iota(jnp.int32, (CHUNK, CHUNK), 1)
    tril = jnp.where(rows >= cols, 1.0, 0.0).astype(BF16)
    sub_rows = lax.broadcasted_iota(jnp.int32, (SUB, dd), 0)

    def chunk_body(ci, carry):
        r0 = pl.multiple_of(ci * CHUNK, CHUNK)
        q = q_ref[0, pl.ds(r0, CHUNK), :]
        z = f_ref[0, pl.ds(r0, CHUNK), :]
        v = i_ref[0, pl.ds(r0, CHUNK), :]
        e = jnp.exp(-jnp.abs(z))
        log_sig = jnp.minimum(z, 0.0) - jnp.log1p(e)
        bb = log1m_lb + log_sig
        lf = jnp.maximum(log_lb, bb) + jnp.log1p(jnp.exp(-jnp.abs(log_lb - bb)))
        k = one_m_lb * (jnp.where(z >= 0.0, e, 1.0) / (1.0 + e))
        hi, mid, lo = _split3(lf)
        bc = (jnp.dot(tril, hi, preferred_element_type=F32)
              + jnp.dot(tril, mid, preferred_element_type=F32)
              + jnp.dot(tril, lo, preferred_element_type=F32))
        bl = bc[CHUNK - 1:CHUNK, :]
        bc_s[...] = bc
        k_s[...] = k
        v_s[...] = v
        qg = (q * jnp.exp(bc)).astype(BF16)
        kg = (k * jnp.exp(bl - bc)).astype(BF16)
        eb = jnp.exp(bl)
        vb = v.astype(BF16)
        for h in range(nh):
            sl = slice(h * REC_D, (h + 1) * REC_D)
            st = st_ref[h]
            oi_s[:, sl] = lax.dot_general(qg[:, sl], st.astype(BF16), (((1,), (1,)), ((), ())),
                                          preferred_element_type=F32)
            upd = lax.dot_general(vb[:, sl], kg[:, sl], (((0,), (0,)), ((), ())),
                                  preferred_element_type=F32)
            st_ref[h] = st * eb[:, sl] + upd
        for i in range(CHUNK // SUB):
            rs = slice(i * SUB, (i + 1) * SUB)
            qi = q[rs, :]
            bi = bc[rs, :]
            acc = oi_s[rs, :]
            if i > 0:
                ref = bc_s[i * SUB - 1:i * SUB, :]
                qf = (qi * jnp.exp(bi - ref)).astype(BF16)
                kf = (k_s[0:i * SUB, :] * jnp.exp(ref - bc_s[0:i * SUB, :])).astype(BF16)
                vf = v_s[0:i * SUB, :].astype(BF16)
                parts = []
                for h in range(nh):
                    sl = slice(h * REC_D, (h + 1) * REC_D)
                    a = lax.dot_general(qf[:, sl], kf[:, sl], (((1,), (1,)), ((), ())),
                                        preferred_element_type=F32)
                    parts.append(jnp.dot(a.astype(BF16), vf[:, sl], preferred_element_type=F32))
                acc = acc + jnp.concatenate(parts, axis=-1)
            for j in range(SUB):
                s = i * SUB + j
                diff = jnp.where(sub_rows >= j, bi - bc_s[s:s + 1, :], NEG_BIG)
                pr = qi * jnp.exp(diff) * k_s[s:s + 1, :]
                vrow = v_s[s:s + 1, :]
                parts = []
                for h in range(nh):
                    sl = slice(h * REC_D, (h + 1) * REC_D)
                    a = jnp.sum(pr[:, sl], axis=-1, keepdims=True)
                    parts.append(a * vrow[:, sl])
                acc = acc + jnp.concatenate(parts, axis=-1)
            oi_s[rs, :] = acc
        o = oi_s[...]
        g = g_ref[0, pl.ds(r0, CHUNK), :]
        parts = []
        for h in range(nh):
            sl = slice(h * REC_D, (h + 1) * REC_D)
            oh = o[:, sl]
            ms = jnp.mean(oh * oh, axis=-1, keepdims=True)
            parts.append(oh * lax.rsqrt(ms + LN_EPS) * ng_ref[...])
        y = jnp.concatenate(parts, axis=-1) * _silu(g)
        o_ref[0, pl.ds(r0, CHUNK), :] = y.astype(o_ref.dtype)
        return carry

    lax.fori_loop(0, tb // CHUNK, chunk_body, 0)


def _rec_mixer(p, lb, norm_g, col0, tb):
    b, t, _ = p.shape
    dd = lb.shape[0]
    nh = dd // REC_D
    spec = lambda c: pl.BlockSpec((1, tb, dd), lambda i, j, c=c: (i, j, c))
    return pl.pallas_call(
        _rec_kernel,
        out_shape=jax.ShapeDtypeStruct((b, t, dd), BF16),
        grid=(b, t // tb),
        in_specs=[spec(col0), spec(col0 + 1), spec(col0 + 2), spec(col0 + 3),
                  pl.BlockSpec((1, dd), lambda i, j: (0, 0)),
                  pl.BlockSpec((1, REC_D), lambda i, j: (0, 0))],
        out_specs=pl.BlockSpec((1, tb, dd), lambda i, j: (i, j, 0)),
        scratch_shapes=[pltpu.VMEM((nh, REC_D, REC_D), F32),
                        pltpu.VMEM((CHUNK, dd), F32),
                        pltpu.VMEM((CHUNK, dd), F32),
                        pltpu.VMEM((CHUNK, dd), F32),
                        pltpu.VMEM((CHUNK, dd), F32)],
        compiler_params=_cparams(("arbitrary", "arbitrary")),
        name="rec_mixer",
    )(p, p, p, p, lb.reshape(1, dd), norm_g.reshape(1, REC_D))


def _att_kernel(q_ref, k_ref, v_ref, bias_ref, o_ref, kbuf, vbuf):
    qb = q_ref.shape[1]
    da = q_ref.shape[2]
    hist = kbuf.shape[0] - qb
    t = pl.program_id(1)

    @pl.when(t == 0)
    def _():
        kbuf[0:hist, :] = jnp.zeros((hist, da), BF16)
        vbuf[0:hist, :] = jnp.zeros((hist, da), BF16)

    @pl.when(t > 0)
    def _():
        kbuf[0:hist, :] = kbuf[qb:qb + hist, :]
        vbuf[0:hist, :] = vbuf[qb:qb + hist, :]

    kbuf[hist:hist + qb, :] = k_ref[0].astype(BF16)
    vbuf[hist:hist + qb, :] = v_ref[0].astype(BF16)

    q = q_ref[0] * (ATT_DH ** -0.5)
    lane_head = lax.broadcasted_iota(jnp.int32, (qb, da), 1) // ATT_DH
    kidx = lax.broadcasted_iota(jnp.int32, (qb, hist + qb), 1)
    in_seq = kidx >= hist - t * qb
    kk = kbuf[...]
    vv = vbuf[...]
    out = jnp.zeros((qb, da), F32)
    for h in range(da // ATT_DH):
        qm = jnp.where(lane_head == h, q, 0.0).astype(BF16)
        s = lax.dot_general(qm, kk, (((1,), (1,)), ((), ())), preferred_element_type=F32)
        s = jnp.where(in_seq, s + bias_ref[h], NEG_BIG)
        m = jnp.max(s, axis=-1, keepdims=True)
        e = jnp.exp(s - m)
        l = jnp.sum(e, axis=-1, keepdims=True)
        oh = jnp.dot(e.astype(BF16), vv, preferred_element_type=F32)
        out = out + jnp.where(lane_head == h, oh / l, 0.0)
    o_ref[0] = out.astype(o_ref.dtype)


def _att_bias(rel_table, qb):
    hist = ATT_LEFT * CHUNK
    qpos = np.arange(qb)[:, None]
    kpos = np.arange(qb + hist)[None, :] - hist
    idx = np.clip(kpos - qpos, -MAX_REL, MAX_REL) + MAX_REL
    dchunk = (kpos + hist) // CHUNK - (qpos + hist) // CHUNK
    band = (dchunk <= 0) & (dchunk >= -ATT_LEFT)
    bias = rel_table.astype(F32)[:, idx]
    return jnp.where(band[None], bias, NEG_BIG)


def _att_mixer(p, rel_table, col0, qb):
    b, t, _ = p.shape
    da = ATT_HEADS * ATT_DH
    hist = ATT_LEFT * CHUNK
    bias = _att_bias(rel_table, qb)
    spec = lambda c: pl.BlockSpec((1, qb, da), lambda i, j, c=c: (i, j, c))
    return pl.pallas_call(
        _att_kernel,
        out_shape=jax.ShapeDtypeStruct((b, t, da), BF16),
        grid=(b, t // qb),
        in_specs=[spec(col0), spec(col0 + 1), spec(col0 + 2),
                  pl.BlockSpec((ATT_HEADS, qb, hist + qb), lambda i, j: (0, 0, 0))],
        out_specs=pl.BlockSpec((1, qb, da), lambda i, j: (i, j, 0)),
        scratch_shapes=[pltpu.VMEM((hist + qb, da), BF16),
                        pltpu.VMEM((hist + qb, da), BF16)],
        compiler_params=_cparams(("arbitrary", "arbitrary")),
        name="att_mixer",
    )(p, p, p, bias)


def _outproj_kernel(x_ref, yc_ref, yr_ref, ya_ref, w_ref, gate_ref, g_ref, b_ref, o_ref, *, alpha):
    y = jnp.concatenate([yc_ref[0], yr_ref[0], ya_ref[0]], axis=-1)
    y = jnp.dot(y, w_ref[...], preferred_element_type=F32)
    r = alpha * x_ref[0] + (1.0 + gate_ref[0]) * y
    o_ref[0] = _layer_norm(r, g_ref[...], b_ref[...])


def _out_proj(x, yc, yr, ya, w_bf16, gate, ln_g, ln_b, alpha, tm):
    b, t, d = x.shape
    blk = lambda a: pl.BlockSpec((1, tm, a.shape[2]), lambda i, j: (i, j, 0))
    return pl.pallas_call(
        functools.partial(_outproj_kernel, alpha=alpha),
        out_shape=jax.ShapeDtypeStruct((b, t, d), F32),
        grid=(b, t // tm),
        in_specs=[blk(x), blk(yc), blk(yr), blk(ya),
                  pl.BlockSpec(w_bf16.shape, lambda i, j: (0, 0)),
                  pl.BlockSpec((1, 1, d), lambda i, j: (i, 0, 0)),
                  pl.BlockSpec((1, d), lambda i, j: (0, 0)),
                  pl.BlockSpec((1, d), lambda i, j: (0, 0))],
        out_specs=blk(x),
        compiler_params=_cparams(("arbitrary", "arbitrary")),
        name="out_proj",
    )(x, yc, yr, ya, w_bf16, gate, ln_g.reshape(1, d), ln_b.reshape(1, d))


def _ffn_kernel(x_ref, sc_ref, sh_ref, gate_ref, w1_ref, w2_ref, g_ref, b_ref, o_ref, a_s,
                *, alpha, d_ff, fc):
    x = x_ref[0]
    h = (x * (1.0 + sc_ref[0]) + sh_ref[0]).astype(BF16)
    for c in range(d_ff // fc):
        gt = jnp.dot(h, w1_ref[:, c * fc:(c + 1) * fc], preferred_element_type=F32)
        up = jnp.dot(h, w1_ref[:, d_ff + c * fc:d_ff + (c + 1) * fc], preferred_element_type=F32)
        a_s[:, c * fc:(c + 1) * fc] = (_silu(gt) * up).astype(BF16)
    y = jnp.dot(a_s[...], w2_ref[...], preferred_element_type=F32)
    r = alpha * x + (1.0 + gate_ref[0]) * y
    o_ref[0] = _layer_norm(r, g_ref[...], b_ref[...])


def _ffn(x, sc, sh, gate, w1_bf16, w2_bf16, ln_g, ln_b, alpha, tm):
    b, t, d = x.shape
    d_ff = w2_bf16.shape[0]
    fc = 256
    mod = pl.BlockSpec((1, 1, d), lambda i, j: (i, 0, 0))
    return pl.pallas_call(
        functools.partial(_ffn_kernel, alpha=alpha, d_ff=d_ff, fc=fc),
        out_shape=jax.ShapeDtypeStruct((b, t, d), F32),
        grid=(b, t // tm),
        in_specs=[pl.BlockSpec((1, tm, d), lambda i, j: (i, j, 0)),
                  mod, mod, mod,
                  pl.BlockSpec(w1_bf16.shape, lambda i, j: (0, 0), pipeline_mode=pl.Buffered(1)),
                  pl.BlockSpec(w2_bf16.shape, lambda i, j: (0, 0), pipeline_mode=pl.Buffered(1)),
                  pl.BlockSpec((1, d), lambda i, j: (0, 0)),
                  pl.BlockSpec((1, d), lambda i, j: (0, 0))],
        out_specs=pl.BlockSpec((1, tm, d), lambda i, j: (i, j, 0)),
        scratch_shapes=[pltpu.VMEM((tm, d_ff), BF16)],
        compiler_params=_cparams(("arbitrary", "arbitrary")),
        name="ffn",
    )(x, sc, sh, gate, w1_bf16, w2_bf16, ln_g.reshape(1, d), ln_b.reshape(1, d))


def kernel(x, c, w_ada, b_ada, w_in, conv_w, conv_b, conv_ln_g, conv_ln_b, rec_lower_bound,
           rec_norm_g, rel_bias, w_out, ln1_g, ln1_b, ln2_g, ln2_b, w_ffn_in, w_ffn_out):
    depth = w_in.shape[0]
    b, t, d = x.shape
    d_conv = conv_w.shape[2]
    d_rec = rec_lower_bound.shape[1]
    d_att = ATT_HEADS * ATT_DH
    alpha = (2 * depth) ** 0.25

    lbs = jax.nn.softmax(rec_lower_bound.astype(F32), axis=0)
    lbs = jnp.cumsum(lbs, axis=0) - lbs[0]
    mods = _ada_mod(c, w_ada, b_ada)

    rec_col0 = (2 * d_conv) // d_rec
    att_col0 = (2 * d_conv + 4 * d_rec) // d_att
    for l in range(depth):
        sh1, sc1, g1, sh2, sc2, g2 = [m.reshape(b, 1, d) for m in jnp.split(mods[l], 6, axis=-1)]
        p = _in_proj(x, sc1, sh1, w_in[l].astype(BF16), tm=512)
        y_conv = _conv_mixer(p, conv_w[l], conv_b[l], conv_ln_g[l], conv_ln_b[l], tb=512)
        y_rec = _rec_mixer(p, lbs[l], rec_norm_g[l], rec_col0, tb=256)
        y_att = _att_mixer(p, rel_bias[l], att_col0, qb=ATT_QB)
        x = _out_proj(x, y_conv, y_rec, y_att, w_out[l].astype(BF16), g1, ln1_g[l], ln1_b[l],
                      alpha, tm=512)
        x = _ffn(x, sc2, sh2, g2, w_ffn_in[l].astype(BF16), w_ffn_out[l].astype(BF16),
                 ln2_g[l], ln2_b[l], alpha, tm=512)
    return x
```

```python
import functools
import math

import numpy as np
import jax
import jax.numpy as jnp
from jax import lax
from jax.experimental import pallas as pl
from jax.experimental.pallas import tpu as pltpu

F32 = jnp.float32
BF16 = jnp.bfloat16

CHUNK = 64
CONV_WIDTH = 31
REC_HEADS = 4
REC_D = 128
ATT_HEADS = 4
ATT_DH = 64
ATT_LEFT = 8
MAX_REL = 128
LN_EPS = 1e-5
NEG_BIG = -1e30
TINY = 1e-30

ATT_QB = 256
CONV_TAIL = 32
CONV_RS = 64

VMEM_LIMIT = 56 * 1024 * 1024


def _cparams(sem):
    return pltpu.CompilerParams(dimension_semantics=sem, vmem_limit_bytes=VMEM_LIMIT)


def _layer_norm(x, g, b):
    mu = jnp.mean(x, axis=-1, keepdims=True)
    xc = x - mu
    var = jnp.mean(xc * xc, axis=-1, keepdims=True)
    return xc * lax.rsqrt(var + LN_EPS) * g + b


def _silu(x):
    return x * jax.nn.sigmoid(x)


def _ada_kernel(c_ref, w_ref, b_ref, o_ref):
    ca = _silu(c_ref[...])
    o_ref[0] = jnp.dot(ca, w_ref[0], precision=lax.Precision.HIGHEST,
                       preferred_element_type=F32) + b_ref[0]


def _ada_mod(c, w_ada, b_ada):
    depth, d, n6 = w_ada.shape
    b = c.shape[0]
    bp = ((b + 7) // 8) * 8
    cp = jnp.zeros((bp, d), F32).at[:b].set(c)
    tn = 1536
    out = pl.pallas_call(
        _ada_kernel,
        out_shape=jax.ShapeDtypeStruct((depth, bp, n6), F32),
        grid=(depth, n6 // tn),
        in_specs=[pl.BlockSpec((bp, d), lambda l, j: (0, 0)),
                  pl.BlockSpec((1, d, tn), lambda l, j: (l, 0, j)),
                  pl.BlockSpec((1, 1, tn), lambda l, j: (l, 0, j))],
        out_specs=pl.BlockSpec((1, bp, tn), lambda l, j: (l, 0, j)),
        compiler_params=_cparams(("arbitrary", "arbitrary")),
        name="ada_mod",
    )(cp, w_ada, b_ada.reshape(depth, 1, n6))
    return out[:, :b]


def _inproj_kernel(x_ref, sc_ref, sh_ref, w_ref, o_ref):
    h = x_ref[0] * (1.0 + sc_ref[0]) + sh_ref[0]
    o_ref[0] = jnp.dot(h.astype(BF16), w_ref[...], preferred_element_type=F32)


def _in_proj(x, sc, sh, w_bf16, tm):
    b, t, d = x.shape
    n = w_bf16.shape[1]
    return pl.pallas_call(
        _inproj_kernel,
        out_shape=jax.ShapeDtypeStruct((b, t, n), F32),
        grid=(b, t // tm),
        in_specs=[pl.BlockSpec((1, tm, d), lambda i, j: (i, j, 0)),
                  pl.BlockSpec((1, 1, d), lambda i, j: (i, 0, 0)),
                  pl.BlockSpec((1, 1, d), lambda i, j: (i, 0, 0)),
                  pl.BlockSpec((d, n), lambda i, j: (0, 0))],
        out_specs=pl.BlockSpec((1, tm, n), lambda i, j: (i, j, 0)),
        compiler_params=_cparams(("arbitrary", "arbitrary")),
        name="in_proj",
    )(x, sc, sh, w_bf16)


def _conv_kernel(av_ref, ag_ref, w_ref, b_ref, g_ref, be_ref, o_ref, ubuf):
    tb = av_ref.shape[1]
    t = pl.program_id(1)

    @pl.when(t == 0)
    def _():
        ubuf[0:CONV_TAIL, :] = jnp.zeros((CONV_TAIL, ubuf.shape[1]), F32)

    @pl.when(t > 0)
    def _():
        ubuf[0:CONV_TAIL, :] = ubuf[tb:tb + CONV_TAIL, :]

    ubuf[CONV_TAIL:CONV_TAIL + tb, :] = av_ref[0] * jax.nn.sigmoid(ag_ref[0])
    off = CONV_TAIL - (CONV_WIDTH - 1)
    for r in range(tb // CONV_RS):
        acc = jnp.zeros((CONV_RS, ubuf.shape[1]), F32)
        for j in range(CONV_WIDTH):
            s = r * CONV_RS + off + j
            acc = acc + w_ref[j:j + 1, :] * ubuf[s:s + CONV_RS, :]
        y = _layer_norm(acc + b_ref[...], g_ref[...], be_ref[...])
        o_ref[0, r * CONV_RS:(r + 1) * CONV_RS, :] = _silu(y).astype(o_ref.dtype)


def _conv_mixer(p, w, bias, ln_g, ln_b, tb):
    b, t, _ = p.shape
    c = w.shape[1]
    row = lambda a: a.reshape(1, c)
    return pl.pallas_call(
        _conv_kernel,
        out_shape=jax.ShapeDtypeStruct((b, t, c), BF16),
        grid=(b, t // tb),
        in_specs=[pl.BlockSpec((1, tb, c), lambda i, j: (i, j, 0)),
                  pl.BlockSpec((1, tb, c), lambda i, j: (i, j, 1)),
                  pl.BlockSpec((CONV_WIDTH, c), lambda i, j: (0, 0)),
                  pl.BlockSpec((1, c), lambda i, j: (0, 0)),
                  pl.BlockSpec((1, c), lambda i, j: (0, 0)),
                  pl.BlockSpec((1, c), lambda i, j: (0, 0))],
        out_specs=pl.BlockSpec((1, tb, c), lambda i, j: (i, j, 0)),
        scratch_shapes=[pltpu.VMEM((tb + CONV_TAIL, c), F32)],
        compiler_params=_cparams(("arbitrary", "arbitrary")),
        name="conv_mixer",
    )(p, p, w, row(bias), row(ln_g), row(ln_b))


REC_LEVELS = tuple(CHUNK >> (i + 1) for i in range(int(math.log2(CHUNK))))
REC_SAFE_RANGE = 150.0
REC_GATE_ROWS = 16


def _rec_tables():
    x = np.arange(CHUNK)
    sel = [x[:, None] >= x[None, :]]
    masks = []
    for m in REC_LEVELS:
        bnd = (x // (2 * m)) * (2 * m) + m - 1
        sel.append(bnd[:, None] >= x[None, :])
        same = (x[:, None] // (2 * m)) == (x[None, :] // (2 * m))
        masks.append(same & ((x[:, None] & m) != 0) & ((x[None, :] & m) == 0))
    masks.append(x[:, None] == x[None, :])
    sel = np.concatenate(sel, axis=0).astype(np.float32)
    return np.concatenate([sel, sel, sel], axis=1), np.stack(masks).astype(np.float32)


def _split3(x):
    hi = x.astype(BF16)
    r1 = x - hi.astype(F32)
    mid = r1.astype(BF16)
    lo = (r1 - mid.astype(F32)).astype(BF16)
    return hi, mid, lo


def _rec_kernel(q_ref, f_ref, i_ref, g_ref, lb_ref, ng_ref, sel_ref, pm_ref, o_ref,
                st_ref, lf_s, k_s):
    tb = q_ref.shape[1]
    dd = q_ref.shape[2]
    nh = dd // REC_D
    nchunks = tb // CHUNK
    nt = (((1,), (1,)), ((), ()))
    tn = (((0,), (0,)), ((), ()))

    @pl.when(pl.program_id(1) == 0)
    def _():
        st_ref[...] = jnp.zeros(st_ref.shape, F32)

    lb = lb_ref[...]
    lb_floor = jnp.maximum(lb, TINY)
    one_m_lb = 1.0 - lb

    def gate_chunk(c, lowest):
        tot = jnp.zeros((1, dd), F32)
        for j in range(CHUNK // REC_GATE_ROWS):
            rs = pl.ds(pl.multiple_of(c * CHUNK + j * REC_GATE_ROWS, REC_GATE_ROWS), REC_GATE_ROWS)
            z = f_ref[0, rs, :]
            e = jnp.exp(-jnp.abs(z))
            r = 1.0 / (1.0 + e)
            pos = z >= 0.0
            lf = jnp.log(lb_floor + one_m_lb * (jnp.where(pos, 1.0, e) * r))
            lf_s[rs, :] = lf
            k_s[rs, :] = one_m_lb * (jnp.where(pos, e, 1.0) * r)
            tot = tot + jnp.sum(lf, axis=0, keepdims=True)
        return jnp.minimum(lowest, tot)

    lowest = lax.fori_loop(0, nchunks, gate_chunk, jnp.zeros((1, dd), F32))
    safe = jnp.min(lowest) >= -REC_SAFE_RANGE

    def prefix(lf, nrows):
        return jnp.dot(sel_ref[0:nrows, :], jnp.concatenate(_split3(lf), axis=0),
                       preferred_element_type=F32)

    def head_out(h, a, qg, kg, vb, eb):
        sl = slice(h * REC_D, (h + 1) * REC_D)
        st = st_ref[h]
        o_h = (lax.dot_general(qg[:, sl], st.astype(BF16), nt, preferred_element_type=F32)
               + jnp.dot(a.astype(BF16), vb[:, sl], preferred_element_type=F32))
        upd = lax.dot_general(vb[:, sl], kg[:, sl], tn, preferred_element_type=F32)
        st_ref[h] = st * eb[:, sl] + upd
        ms = jnp.mean(o_h * o_h, axis=-1, keepdims=True)
        return o_h * lax.rsqrt(ms + LN_EPS) * ng_ref[...]

    def store(rs, outs):
        y = jnp.concatenate(outs, axis=-1) * _silu(g_ref[0, rs, :])
        o_ref[0, rs, :] = y.astype(o_ref.dtype)

    @pl.when(safe)
    def _():
        causal = (lax.broadcasted_iota(jnp.int32, (CHUNK, CHUNK), 0)
                  >= lax.broadcasted_iota(jnp.int32, (CHUNK, CHUNK), 1))
        heads = [slice(h * REC_D, (h + 1) * REC_D) for h in range(nh)]
        rows = [slice(c * CHUNK, (c + 1) * CHUNK) for c in range(nchunks)]
        bcs = [prefix(lf_s[rs, :], CHUNK) for rs in rows]
        qg, kg, vb, eb, att, upd = [], [], [], [], [], []
        for rs, bc in zip(rows, bcs):
            half = 0.5 * bc[CHUNK - 1:CHUNK, :]
            eh = jnp.exp(half)
            qf = q_ref[0, rs, :] * jnp.exp(bc - half)
            kf = k_s[rs, :] * jnp.exp(half - bc)
            qi = qf.astype(BF16)
            ki = kf.astype(BF16)
            qg.append((qf * eh).astype(BF16))
            kg.append((kf * eh).astype(BF16))
            vb.append(i_ref[0, rs, :].astype(BF16))
            eb.append(eh * eh)
            att.append([jnp.where(causal, lax.dot_general(qi[:, sl], ki[:, sl], nt,
                                                          preferred_element_type=F32),
                                  0.0).astype(BF16) for sl in heads])
        for c in range(nchunks):
            upd.append([lax.dot_general(vb[c][:, sl], kg[c][:, sl], tn, preferred_element_type=F32)
                        for sl in heads])
        states = []
        for h, sl in enumerate(heads):
            st = st_ref[h]
            per_chunk = []
            for c in range(nchunks):
                per_chunk.append(st.astype(BF16))
                st = st * eb[c][:, sl] + upd[c][h]
            st_ref[h] = st
            states.append(per_chunk)
        for c, rs in enumerate(rows):
            outs = []
            for h, sl in enumerate(heads):
                o_h = (lax.dot_general(qg[c][:, sl], states[h][c], nt, preferred_element_type=F32)
                       + jnp.dot(att[c][h], vb[c][:, sl], preferred_element_type=F32))
                ms = jnp.mean(o_h * o_h, axis=-1, keepdims=True)
                outs.append(o_h * lax.rsqrt(ms + LN_EPS) * ng_ref[...])
            store(rs, outs)

    @pl.when(jnp.logical_not(safe))
    def _():
        row_id = lax.broadcasted_iota(jnp.int32, (CHUNK, dd), 0)

        def chunk_body(ci, carry):
            rs = pl.ds(pl.multiple_of(ci * CHUNK, CHUNK), CHUNK)
            q = q_ref[0, rs, :]
            k = k_s[rs, :]
            pre = prefix(lf_s[rs, :], sel_ref.shape[0])
            bc = pre[0:CHUNK, :]
            bl = bc[CHUNK - 1:CHUNK, :]
            qg = (q * jnp.exp(bc)).astype(BF16)
            kg = (k * jnp.exp(bl - bc)).astype(BF16)
            vb = i_ref[0, rs, :].astype(BF16)
            xs = []
            for li, m in enumerate(REC_LEVELS):
                dec = jnp.exp(-jnp.abs(bc - pre[(li + 1) * CHUNK:(li + 2) * CHUNK, :]))
                xs.append((jnp.where((row_id & m) != 0, q, k) * dec).astype(BF16))
            qb16 = q.astype(BF16)
            kb16 = k.astype(BF16)
            outs = []
            for h in range(nh):
                sl = slice(h * REC_D, (h + 1) * REC_D)
                a = (lax.dot_general(qb16[:, sl], kb16[:, sl], nt, preferred_element_type=F32)
                     * pm_ref[len(REC_LEVELS)])
                for li in range(len(REC_LEVELS)):
                    xh = xs[li][:, sl]
                    a = a + lax.dot_general(xh, xh, nt, preferred_element_type=F32) * pm_ref[li]
                outs.append(head_out(h, a, qg, kg, vb, jnp.exp(bl)))
            store(rs, outs)
            return carry

        lax.fori_loop(0, nchunks, chunk_body, 0)


def _rec_mixer(p, lb, norm_g, col0, tb):
    b, t, _ = p.shape
    dd = lb.shape[0]
    nh = dd // REC_D
    sel, pmask = _rec_tables()
    sel = jnp.asarray(sel, BF16)
    pmask = jnp.asarray(pmask, F32)
    spec = lambda c: pl.BlockSpec((1, tb, dd), lambda i, j, c=c: (i, j, c))
    return pl.pallas_call(
        _rec_kernel,
        out_shape=jax.ShapeDtypeStruct((b, t, dd), BF16),
        grid=(b, t // tb),
        in_specs=[spec(col0), spec(col0 + 1), spec(col0 + 2), spec(col0 + 3),
                  pl.BlockSpec((1, dd), lambda i, j: (0, 0)),
                  pl.BlockSpec((1, REC_D), lambda i, j: (0, 0)),
                  pl.BlockSpec(sel.shape, lambda i, j: (0, 0)),
                  pl.BlockSpec(pmask.shape, lambda i, j: (0, 0, 0))],
        out_specs=pl.BlockSpec((1, tb, dd), lambda i, j: (i, j, 0)),
        scratch_shapes=[pltpu.VMEM((nh, REC_D, REC_D), F32),
                        pltpu.VMEM((tb, dd), F32),
                        pltpu.VMEM((tb, dd), F32)],
        compiler_params=_cparams(("arbitrary", "arbitrary")),
        name="rec_mixer",
    )(p, p, p, p, lb.reshape(1, dd), norm_g.reshape(1, REC_D), sel, pmask)


def _att_kernel(q_ref, k_ref, v_ref, g_ref, o_ref, kbuf, vbuf, bias_ref):
    qb = q_ref.shape[1]
    da = q_ref.shape[2]
    hist = kbuf.shape[0] - qb
    t = pl.program_id(1)

    @pl.when((pl.program_id(0) == 0) & (t == 0))
    def _():
        gw = g_ref.shape[2]
        qrow = lax.broadcasted_iota(jnp.int32, (qb, hist + qb), 0)
        kcol = lax.broadcasted_iota(jnp.int32, (qb, hist + qb), 1)
        dchunk = kcol // CHUNK - qrow // CHUNK
        band = (dchunk >= 0) & (dchunk <= ATT_LEFT)
        for h in range(da // ATT_DH):
            gb = jnp.broadcast_to(g_ref[h], (qb, gw))
            skew = pltpu.roll(gb, gw - (qb - 1), 1, stride=1, stride_axis=0)
            bias_ref[h] = jnp.where(band, skew[:, :hist + qb], NEG_BIG)

    @pl.when(t == 0)
    def _():
        kbuf[0:hist, :] = jnp.zeros((hist, da), BF16)
        vbuf[0:hist, :] = jnp.zeros((hist, da), BF16)

    @pl.when(t > 0)
    def _():
        kbuf[0:hist, :] = kbuf[qb:qb + hist, :]
        vbuf[0:hist, :] = vbuf[qb:qb + hist, :]

    kbuf[hist:hist + qb, :] = k_ref[0].astype(BF16)
    vbuf[hist:hist + qb, :] = v_ref[0].astype(BF16)

    q = q_ref[0] * (ATT_DH ** -0.5)
    lane_head = lax.broadcasted_iota(jnp.int32, (qb, da), 1) // ATT_DH
    kidx = lax.broadcasted_iota(jnp.int32, (qb, hist + qb), 1)
    in_seq = kidx >= hist - t * qb
    kk = kbuf[...]
    vv = vbuf[...]
    out = jnp.zeros((qb, da), F32)
    for h in range(da // ATT_DH):
        qm = jnp.where(lane_head == h, q, 0.0).astype(BF16)
        s = lax.dot_general(qm, kk, (((1,), (1,)), ((), ())), preferred_element_type=F32)
        s = jnp.where(in_seq, s + bias_ref[h], NEG_BIG)
        m = jnp.max(s, axis=-1, keepdims=True)
        e = jnp.exp(s - m)
        l = jnp.sum(e, axis=-1, keepdims=True)
        oh = jnp.dot(e.astype(BF16), vv, preferred_element_type=F32)
        out = out + jnp.where(lane_head == h, oh / l, 0.0)
    o_ref[0] = out.astype(o_ref.dtype)


def _att_rel_row(rel_table, qb):
    hist = ATT_LEFT * CHUNK
    gw = pl.next_power_of_2(2 * qb + hist - 1)
    n_lo = qb - 1 + hist - MAX_REL
    n_hi = gw - n_lo - (2 * MAX_REL + 1)
    nh = rel_table.shape[0]
    tab = rel_table.astype(F32)
    row = jnp.concatenate([jnp.broadcast_to(tab[:, :1], (nh, n_lo)), tab,
                           jnp.broadcast_to(tab[:, -1:], (nh, n_hi))], axis=1)
    return row.reshape(nh, 1, gw)


def _att_mixer(p, rel_table, col0, qb):
    b, t, _ = p.shape
    da = ATT_HEADS * ATT_DH
    hist = ATT_LEFT * CHUNK
    grow = _att_rel_row(rel_table, qb)
    spec = lambda c: pl.BlockSpec((1, qb, da), lambda i, j, c=c: (i, j, c))
    return pl.pallas_call(
        _att_kernel,
        out_shape=jax.ShapeDtypeStruct((b, t, da), BF16),
        grid=(b, t // qb),
        in_specs=[spec(col0), spec(col0 + 1), spec(col0 + 2),
                  pl.BlockSpec(grow.shape, lambda i, j: (0, 0, 0))],
        out_specs=pl.BlockSpec((1, qb, da), lambda i, j: (i, j, 0)),
        scratch_shapes=[pltpu.VMEM((hist + qb, da), BF16),
                        pltpu.VMEM((hist + qb, da), BF16),
                        pltpu.VMEM((ATT_HEADS, qb, hist + qb), F32)],
        compiler_params=_cparams(("arbitrary", "arbitrary")),
        name="att_mixer",
    )(p, p, p, grow)


def _outproj_kernel(x_ref, yc_ref, yr_ref, ya_ref, w_ref, gate_ref, g_ref, b_ref, o_ref, *, alpha):
    y = jnp.concatenate([yc_ref[0], yr_ref[0], ya_ref[0]], axis=-1)
    y = jnp.dot(y, w_ref[...], preferred_element_type=F32)
    r = alpha * x_ref[0] + (1.0 + gate_ref[0]) * y
    o_ref[0] = _layer_norm(r, g_ref[...], b_ref[...])


def _out_proj(x, yc, yr, ya, w_bf16, gate, ln_g, ln_b, alpha, tm):
    b, t, d = x.shape
    blk = lambda a: pl.BlockSpec((1, tm, a.shape[2]), lambda i, j: (i, j, 0))
    return pl.pallas_call(
        functools.partial(_outproj_kernel, alpha=alpha),
        out_shape=jax.ShapeDtypeStruct((b, t, d), F32),
        grid=(b, t // tm),
        in_specs=[blk(x), blk(yc), blk(yr), blk(ya),
                  pl.BlockSpec(w_bf16.shape, lambda i, j: (0, 0)),
                  pl.BlockSpec((1, 1, d), lambda i, j: (i, 0, 0)),
                  pl.BlockSpec((1, d), lambda i, j: (0, 0)),
                  pl.BlockSpec((1, d), lambda i, j: (0, 0))],
        out_specs=blk(x),
        compiler_params=_cparams(("arbitrary", "arbitrary")),
        name="out_proj",
    )(x, yc, yr, ya, w_bf16, gate, ln_g.reshape(1, d), ln_b.reshape(1, d))


def _ffn_kernel(x_ref, sc_ref, sh_ref, gate_ref, w1_ref, w2_ref, g_ref, b_ref, o_ref, a_s,
                *, alpha, d_ff, fc):
    x = x_ref[0]
    h = (x * (1.0 + sc_ref[0]) + sh_ref[0]).astype(BF16)
    for c in range(d_ff // fc):
        gt = jnp.dot(h, w1_ref[:, c * fc:(c + 1) * fc], preferred_element_type=F32)
        up = jnp.dot(h, w1_ref[:, d_ff + c * fc:d_ff + (c + 1) * fc], preferred_element_type=F32)
        a_s[:, c * fc:(c + 1) * fc] = (_silu(gt) * up).astype(BF16)
    y = jnp.dot(a_s[...], w2_ref[...], preferred_element_type=F32)
    r = alpha * x + (1.0 + gate_ref[0]) * y
    o_ref[0] = _layer_norm(r, g_ref[...], b_ref[...])


def _ffn(x, sc, sh, gate, w1_bf16, w2_bf16, ln_g, ln_b, alpha, tm):
    b, t, d = x.shape
    d_ff = w2_bf16.shape[0]
    fc = 256
    mod = pl.BlockSpec((1, 1, d), lambda i, j: (i, 0, 0))
    return pl.pallas_call(
        functools.partial(_ffn_kernel, alpha=alpha, d_ff=d_ff, fc=fc),
        out_shape=jax.ShapeDtypeStruct((b, t, d), F32),
        grid=(b, t // tm),
        in_specs=[pl.BlockSpec((1, tm, d), lambda i, j: (i, j, 0)),
                  mod, mod, mod,
                  pl.BlockSpec(w1_bf16.shape, lambda i, j: (0, 0), pipeline_mode=pl.Buffered(1)),
                  pl.BlockSpec(w2_bf16.shape, lambda i, j: (0, 0), pipeline_mode=pl.Buffered(1)),
                  pl.BlockSpec((1, d), lambda i, j: (0, 0)),
                  pl.BlockSpec((1, d), lambda i, j: (0, 0))],
        out_specs=pl.BlockSpec((1, tm, d), lambda i, j: (i, j, 0)),
        scratch_shapes=[pltpu.VMEM((tm, d_ff), BF16)],
        compiler_params=_cparams(("arbitrary", "arbitrary")),
        name="ffn",
    )(x, sc, sh, gate, w1_bf16, w2_bf16, ln_g.reshape(1, d), ln_b.reshape(1, d))


def kernel(x, c, w_ada, b_ada, w_in, conv_w, conv_b, conv_ln_g, conv_ln_b, rec_lower_bound,
           rec_norm_g, rel_bias, w_out, ln1_g, ln1_b, ln2_g, ln2_b, w_ffn_in, w_ffn_out):
    depth = w_in.shape[0]
    b, t, d = x.shape
    d_conv = conv_w.shape[2]
    d_rec = rec_lower_bound.shape[1]
    d_att = ATT_HEADS * ATT_DH
    alpha = (2 * depth) ** 0.25

    lbs = jax.nn.softmax(rec_lower_bound.astype(F32), axis=0)
    lbs = jnp.cumsum(lbs, axis=0) - lbs[0]
    mods = _ada_mod(c, w_ada, b_ada)

    rec_col0 = (2 * d_conv) // d_rec
    att_col0 = (2 * d_conv + 4 * d_rec) // d_att
    for l in range(depth):
        sh1, sc1, g1, sh2, sc2, g2 = [m.reshape(b, 1, d) for m in jnp.split(mods[l], 6, axis=-1)]
        p = _in_proj(x, sc1, sh1, w_in[l].astype(BF16), tm=512)
        y_conv = _conv_mixer(p, conv_w[l], conv_b[l], conv_ln_g[l], conv_ln_b[l], tb=512)
        y_rec = _rec_mixer(p, lbs[l], rec_norm_g[l], rec_col0, tb=256)
        y_att = _att_mixer(p, rel_bias[l], att_col0, qb=ATT_QB)
        x = _out_proj(x, y_conv, y_rec, y_att, w_out[l].astype(BF16), g1, ln1_g[l], ln1_b[l],
                      alpha, tm=512)
        x = _ffn(x, sc2, sh2, g2, w_ffn_in[l].astype(BF16), w_ffn_out[l].astype(BF16),
                 ln2_g[l], ln2_b[l], alpha, tm=512)
    return x
```

```python
import functools
import math

import numpy as np
import jax
import jax.numpy as jnp
from jax import lax
from jax.experimental import pallas as pl
from jax.experimental.pallas import tpu as pltpu

F32 = jnp.float32
BF16 = jnp.bfloat16

CHUNK = 64
CONV_WIDTH = 31
REC_HEADS = 4
REC_D = 128
ATT_HEADS = 4
ATT_DH = 64
ATT_LEFT = 8
MAX_REL = 128
LN_EPS = 1e-5
NEG_BIG = -1e30
TINY = 1e-30
LOG2E = math.log2(math.e)

ATT_QB = 256
CONV_TAIL = 32
CONV_RS = 32
GATE_RS = 16
FFN_FC = 256
SUBLANES = 8

VMEM_LIMIT = 56 * 1024 * 1024


def _cparams(sem):
    return pltpu.CompilerParams(dimension_semantics=sem, vmem_limit_bytes=VMEM_LIMIT)


def _layer_norm(x, g, b):
    mu = jnp.mean(x, axis=-1, keepdims=True)
    xc = x - mu
    var = jnp.mean(xc * xc, axis=-1, keepdims=True)
    return xc * lax.rsqrt(var + LN_EPS) * g + b


def _silu(x):
    return x * jax.nn.sigmoid(x)


def _ada_kernel(c_ref, w_ref, b_ref, o_ref):
    ca = _silu(c_ref[...])
    o_ref[0] = jnp.dot(ca, w_ref[0], precision=lax.Precision.HIGHEST,
                       preferred_element_type=F32) + b_ref[0]


def _ada_mod(c, w_ada, b_ada):
    depth, d, n6 = w_ada.shape
    b = c.shape[0]
    bp = ((b + 7) // 8) * 8
    cp = jnp.zeros((bp, d), F32).at[:b].set(c)
    tn = 1536
    out = pl.pallas_call(
        _ada_kernel,
        out_shape=jax.ShapeDtypeStruct((depth, bp, n6), F32),
        grid=(depth, n6 // tn),
        in_specs=[pl.BlockSpec((bp, d), lambda l, j: (0, 0)),
                  pl.BlockSpec((1, d, tn), lambda l, j: (l, 0, j)),
                  pl.BlockSpec((1, 1, tn), lambda l, j: (l, 0, j))],
        out_specs=pl.BlockSpec((1, bp, tn), lambda l, j: (l, 0, j)),
        compiler_params=_cparams(("arbitrary", "arbitrary")),
        name="ada_mod",
    )(cp, w_ada, b_ada.reshape(depth, 1, n6))
    return out[:, :b]


def _front_kernel(x_ref, sc_ref, sh_ref, w_ref, cw_ref, cb_ref, cg_ref, cbe_ref, lb_ref,
                  yc_ref, rq_ref, lf_ref, rk_ref, ri_ref, rg_ref, aq_ref, ak_ref, av_ref, shs, h_s):
    tm = x_ref.shape[1]
    dc = cw_ref.shape[1]
    dr = lb_ref.shape[1]
    da = aq_ref.shape[2]
    h_s[...] = (x_ref[0] * (1.0 + sc_ref[0]) + sh_ref[0]).astype(BF16)

    def proj(c0, n):
        return jnp.dot(h_s[...], w_ref[:, c0:c0 + n], preferred_element_type=F32)

    @pl.when(pl.program_id(1) == 0)
    def _():
        shs[0, 0:CONV_TAIL, :] = jnp.zeros((CONV_TAIL, dc), F32)

    @pl.when(pl.program_id(1) > 0)
    def _():
        shs[0, 0:CONV_TAIL, :] = shs[0, tm:tm + CONV_TAIL, :]

    pc = proj(0, 2 * dc)
    for r0 in range(0, tm, CONV_RS):
        blk = pc[r0:r0 + CONV_RS, :]
        shs[0, CONV_TAIL + r0:CONV_TAIL + r0 + CONV_RS, :] = blk[:, :dc] * jax.nn.sigmoid(blk[:, dc:])
    n_sh = tm + CONV_TAIL - SUBLANES
    for s in range(1, SUBLANES):
        for r0 in range(0, n_sh, CONV_RS):
            n = min(CONV_RS, n_sh - r0)
            shs[s, r0:r0 + n, :] = shs[0, r0 + s:r0 + s + n, :]

    c0 = 2 * dc
    rq_ref[0] = proj(c0, dr)
    z_all = proj(c0 + dr, dr)
    lb = lb_ref[...]
    lb_floor = jnp.maximum(lb, TINY)
    one_m_lb = 1.0 - lb
    for r0 in range(0, tm, GATE_RS):
        z = z_all[r0:r0 + GATE_RS, :]
        e = jnp.exp(-jnp.abs(z))
        r = 1.0 / (1.0 + e)
        pos = z >= 0.0
        lf_ref[0, r0:r0 + GATE_RS, :] = jnp.log(lb_floor + one_m_lb * (jnp.where(pos, 1.0, e) * r))
        rk_ref[0, r0:r0 + GATE_RS, :] = one_m_lb * (jnp.where(pos, e, 1.0) * r)
    ri_ref[0] = proj(c0 + 2 * dr, dr).astype(BF16)
    rg_ref[0] = proj(c0 + 3 * dr, dr)

    c0 = 2 * dc + 4 * dr
    aq_ref[0] = (proj(c0, da) * (LOG2E * ATT_DH ** -0.5)).astype(BF16)
    ak_ref[0] = proj(c0 + da, da).astype(BF16)
    av_ref[0] = proj(c0 + 2 * da, da).astype(BF16)

    off = CONV_TAIL - (CONV_WIDTH - 1)
    for r0 in range(0, tm, CONV_RS):
        acc = jnp.zeros((CONV_RS, dc), F32)
        for j in range(CONV_WIDTH):
            a, s = divmod(off + j, SUBLANES)
            acc = acc + cw_ref[j:j + 1, :] * shs[s, r0 + a * SUBLANES:r0 + a * SUBLANES + CONV_RS, :]
        y = _layer_norm(acc + cb_ref[...], cg_ref[...], cbe_ref[...])
        yc_ref[0, r0:r0 + CONV_RS, :] = _silu(y).astype(BF16)


def _front(x, sc, sh, w_bf16, conv_w, conv_b, conv_ln_g, conv_ln_b, lb, tm):
    b, t, d = x.shape
    dc = conv_w.shape[1]
    dr = lb.shape[0]
    da = ATT_HEADS * ATT_DH
    row = lambda a: a.reshape(1, -1)
    const = lambda a: pl.BlockSpec(a.shape, lambda i, j: (0,) * a.ndim)
    mod = pl.BlockSpec((1, 1, d), lambda i, j: (i, 0, 0))
    tile = lambda n: pl.BlockSpec((1, tm, n), lambda i, j: (i, j, 0))
    outs = [(dc, BF16), (dr, F32), (dr, F32), (dr, F32), (dr, BF16), (dr, F32),
            (da, BF16), (da, BF16), (da, BF16)]
    consts = [w_bf16, conv_w, row(conv_b), row(conv_ln_g), row(conv_ln_b), row(lb)]
    return pl.pallas_call(
        _front_kernel,
        out_shape=[jax.ShapeDtypeStruct((b, t, n), dt) for n, dt in outs],
        grid=(b, t // tm),
        in_specs=[tile(d), mod, mod] + [const(a) for a in consts],
        out_specs=[tile(n) for n, _ in outs],
        scratch_shapes=[pltpu.VMEM((SUBLANES, tm + CONV_TAIL, dc), F32),
                        pltpu.VMEM((tm, d), BF16)],
        compiler_params=_cparams(("arbitrary", "arbitrary")),
        name="front",
    )(x, sc, sh, *consts)


REC_LEVELS = tuple(CHUNK >> (i + 1) for i in range(int(math.log2(CHUNK))))
REC_SAFE_RANGE = 150.0


def _rec_tables():
    x = np.arange(CHUNK)
    sel = [x[:, None] >= x[None, :]]
    masks = []
    for m in REC_LEVELS:
        bnd = (x // (2 * m)) * (2 * m) + m - 1
        sel.append(bnd[:, None] >= x[None, :])
        same = (x[:, None] // (2 * m)) == (x[None, :] // (2 * m))
        masks.append(same & ((x[:, None] & m) != 0) & ((x[None, :] & m) == 0))
    masks.append(x[:, None] == x[None, :])
    sel = np.concatenate(sel, axis=0).astype(np.float32)
    return np.concatenate([sel, sel, sel], axis=1), np.stack(masks).astype(np.float32)


def _split3(x):
    hi = x.astype(BF16)
    r1 = x - hi.astype(F32)
    mid = r1.astype(BF16)
    lo = (r1 - mid.astype(F32)).astype(BF16)
    return hi, mid, lo


def _rec_kernel(q_ref, lf_ref, k_ref, i_ref, g_ref, ng_ref, sel_ref, pm_ref, o_ref, st_ref):
    tb = q_ref.shape[1]
    dd = q_ref.shape[2]
    nh = dd // REC_D
    nchunks = tb // CHUNK
    nt = (((1,), (1,)), ((), ()))
    tn = (((0,), (0,)), ((), ()))

    @pl.when(pl.program_id(1) == 0)
    def _():
        st_ref[...] = jnp.zeros(st_ref.shape, F32)

    tot = [jnp.sum(lf_ref[0, c * CHUNK:(c + 1) * CHUNK, :], axis=0, keepdims=True)
           for c in range(nchunks)]
    safe = jnp.min(functools.reduce(jnp.minimum, tot)) >= -REC_SAFE_RANGE

    def prefix(lf, nrows):
        return jnp.dot(sel_ref[0:nrows, :], jnp.concatenate(_split3(lf), axis=0),
                       preferred_element_type=F32)

    def head_out(h, a, qg, kg, vb, eb):
        sl = slice(h * REC_D, (h + 1) * REC_D)
        st = st_ref[h]
        o_h = (lax.dot_general(qg[:, sl], st.astype(BF16), nt, preferred_element_type=F32)
               + jnp.dot(a.astype(BF16), vb[:, sl], preferred_element_type=F32))
        upd = lax.dot_general(vb[:, sl], kg[:, sl], tn, preferred_element_type=F32)
        st_ref[h] = st * eb[:, sl] + upd
        ms = jnp.mean(o_h * o_h, axis=-1, keepdims=True)
        return o_h * lax.rsqrt(ms + LN_EPS) * ng_ref[...]

    def store(rs, outs):
        y = jnp.concatenate(outs, axis=-1) * _silu(g_ref[0, rs, :])
        o_ref[0, rs, :] = y.astype(o_ref.dtype)

    @pl.when(safe)
    def _():
        causal = (lax.broadcasted_iota(jnp.int32, (CHUNK, CHUNK), 0)
                  >= lax.broadcasted_iota(jnp.int32, (CHUNK, CHUNK), 1))
        heads = [slice(h * REC_D, (h + 1) * REC_D) for h in range(nh)]
        rows = [slice(c * CHUNK, (c + 1) * CHUNK) for c in range(nchunks)]
        bcs = [prefix(lf_ref[0, rs, :], CHUNK) for rs in rows]
        qg, kg, vb, eb, att, upd = [], [], [], [], [], []
        for rs, bc in zip(rows, bcs):
            half = 0.5 * bc[CHUNK - 1:CHUNK, :]
            eh = jnp.exp(half)
            qf = q_ref[0, rs, :] * jnp.exp(bc - half)
            kf = k_ref[0, rs, :] * jnp.exp(half - bc)
            qi = qf.astype(BF16)
            ki = kf.astype(BF16)
            qg.append((qf * eh).astype(BF16))
            kg.append((kf * eh).astype(BF16))
            vb.append(i_ref[0, rs, :])
            eb.append(eh * eh)
            att.append([jnp.where(causal, lax.dot_general(qi[:, sl], ki[:, sl], nt,
                                                          preferred_element_type=F32),
                                  0.0).astype(BF16) for sl in heads])
        for c in range(nchunks):
            upd.append([lax.dot_general(vb[c][:, sl], kg[c][:, sl], tn, preferred_element_type=F32)
                        for sl in heads])
        states = []
        for h, sl in enumerate(heads):
            st = st_ref[h]
            per_chunk = []
            for c in range(nchunks):
                per_chunk.append(st.astype(BF16))
                st = st * eb[c][:, sl] + upd[c][h]
            st_ref[h] = st
            states.append(per_chunk)
        for c, rs in enumerate(rows):
            outs = []
            for h, sl in enumerate(heads):
                o_h = (lax.dot_general(qg[c][:, sl], states[h][c], nt, preferred_element_type=F32)
                       + jnp.dot(att[c][h], vb[c][:, sl], preferred_element_type=F32))
                ms = jnp.mean(o_h * o_h, axis=-1, keepdims=True)
                outs.append(o_h * lax.rsqrt(ms + LN_EPS) * ng_ref[...])
            store(rs, outs)

    @pl.when(jnp.logical_not(safe))
    def _():
        row_id = lax.broadcasted_iota(jnp.int32, (CHUNK, dd), 0)

        def chunk_body(ci, carry):
            rs = pl.ds(pl.multiple_of(ci * CHUNK, CHUNK), CHUNK)
            q = q_ref[0, rs, :]
            k = k_ref[0, rs, :]
            pre = prefix(lf_ref[0, rs, :], sel_ref.shape[0])
            bc = pre[0:CHUNK, :]
            bl = bc[CHUNK - 1:CHUNK, :]
            qg = (q * jnp.exp(bc)).astype(BF16)
            kg = (k * jnp.exp(bl - bc)).astype(BF16)
            vb = i_ref[0, rs, :]
            xs = []
            for li, m in enumerate(REC_LEVELS):
                dec = jnp.exp(-jnp.abs(bc - pre[(li + 1) * CHUNK:(li + 2) * CHUNK, :]))
                xs.append((jnp.where((row_id & m) != 0, q, k) * dec).astype(BF16))
            qb16 = q.astype(BF16)
            kb16 = k.astype(BF16)
            outs = []
            for h in range(nh):
                sl = slice(h * REC_D, (h + 1) * REC_D)
                a = (lax.dot_general(qb16[:, sl], kb16[:, sl], nt, preferred_element_type=F32)
                     * pm_ref[len(REC_LEVELS)])
                for li in range(len(REC_LEVELS)):
                    xh = xs[li][:, sl]
                    a = a + lax.dot_general(xh, xh, nt, preferred_element_type=F32) * pm_ref[li]
                outs.append(head_out(h, a, qg, kg, vb, jnp.exp(bl)))
            store(rs, outs)
            return carry

        lax.fori_loop(0, nchunks, chunk_body, 0)


def _rec_mixer(q, lf, k, v, g, norm_g, tb):
    b, t, dd = q.shape
    nh = dd // REC_D
    sel, pmask = _rec_tables()
    sel = jnp.asarray(sel, BF16)
    pmask = jnp.asarray(pmask, F32)
    tile = pl.BlockSpec((1, tb, dd), lambda i, j: (i, j, 0))
    return pl.pallas_call(
        _rec_kernel,
        out_shape=jax.ShapeDtypeStruct((b, t, dd), BF16),
        grid=(b, t // tb),
        in_specs=[tile, tile, tile, tile, tile,
                  pl.BlockSpec((1, REC_D), lambda i, j: (0, 0)),
                  pl.BlockSpec(sel.shape, lambda i, j: (0, 0)),
                  pl.BlockSpec(pmask.shape, lambda i, j: (0, 0, 0))],
        out_specs=tile,
        scratch_shapes=[pltpu.VMEM((nh, REC_D, REC_D), F32)],
        compiler_params=_cparams(("arbitrary", "arbitrary")),
        name="rec_mixer",
    )(q, lf, k, v, g, norm_g.reshape(1, REC_D), sel, pmask)


def _att_kernel(q_ref, k_ref, v_ref, g_ref, o_ref, kbuf, vbuf, bias_ref):
    qb = q_ref.shape[1]
    da = q_ref.shape[2]
    hist = kbuf.shape[0] - qb
    t = pl.program_id(1)

    @pl.when((pl.program_id(0) == 0) & (t == 0))
    def _():
        gw = g_ref.shape[2]
        qrow = lax.broadcasted_iota(jnp.int32, (qb, hist + qb), 0)
        kcol = lax.broadcasted_iota(jnp.int32, (qb, hist + qb), 1)
        dchunk = kcol // CHUNK - qrow // CHUNK
        band = (dchunk >= 0) & (dchunk <= ATT_LEFT)
        for h in range(da // ATT_DH):
            gb = jnp.broadcast_to(g_ref[h], (qb, gw))
            skew = pltpu.roll(gb, gw - (qb - 1), 1, stride=1, stride_axis=0)
            bias_ref[h] = jnp.where(band, skew[:, :hist + qb] * LOG2E, NEG_BIG)

    @pl.when(t == 0)
    def _():
        kbuf[0:hist, :] = jnp.zeros((hist, da), BF16)
        vbuf[0:hist, :] = jnp.zeros((hist, da), BF16)

    @pl.when(t > 0)
    def _():
        kbuf[0:hist, :] = kbuf[qb:qb + hist, :]
        vbuf[0:hist, :] = vbuf[qb:qb + hist, :]

    kbuf[hist:hist + qb, :] = k_ref[0]
    vbuf[hist:hist + qb, :] = v_ref[0]

    q = q_ref[0]
    lane_head = lax.broadcasted_iota(jnp.int32, (qb, da), 1) // ATT_DH
    kidx = lax.broadcasted_iota(jnp.int32, (qb, hist + qb), 1)
    in_seq = kidx >= hist - t * qb
    kk = kbuf[...]
    vv = vbuf[...]
    nh = da // ATT_DH
    scores = [lax.dot_general(jnp.where(lane_head == h, q, jnp.zeros_like(q)), kk,
                              (((1,), (1,)), ((), ())), preferred_element_type=F32)
              for h in range(nh)]
    out = jnp.zeros((qb, da), F32)
    for h in range(nh):
        s = jnp.where(in_seq, scores[h] + bias_ref[h], NEG_BIG)
        m = jnp.max(s, axis=-1, keepdims=True)
        e = jnp.exp2(s - m)
        l = jnp.sum(e, axis=-1, keepdims=True)
        oh = jnp.dot(e.astype(BF16), vv, preferred_element_type=F32)
        out = out + jnp.where(lane_head == h, oh / l, 0.0)
    o_ref[0] = out.astype(o_ref.dtype)


def _att_rel_row(rel_table, qb):
    hist = ATT_LEFT * CHUNK
    gw = pl.next_power_of_2(2 * qb + hist - 1)
    n_lo = qb - 1 + hist - MAX_REL
    n_hi = gw - n_lo - (2 * MAX_REL + 1)
    nh = rel_table.shape[0]
    tab = rel_table.astype(F32)
    row = jnp.concatenate([jnp.broadcast_to(tab[:, :1], (nh, n_lo)), tab,
                           jnp.broadcast_to(tab[:, -1:], (nh, n_hi))], axis=1)
    return row.reshape(nh, 1, gw)


def _att_mixer(q, k, v, rel_table, qb):
    b, t, da = q.shape
    hist = ATT_LEFT * CHUNK
    grow = _att_rel_row(rel_table, qb)
    tile = pl.BlockSpec((1, qb, da), lambda i, j: (i, j, 0))
    return pl.pallas_call(
        _att_kernel,
        out_shape=jax.ShapeDtypeStruct((b, t, da), BF16),
        grid=(b, t // qb),
        in_specs=[tile, tile, tile, pl.BlockSpec(grow.shape, lambda i, j: (0, 0, 0))],
        out_specs=tile,
        scratch_shapes=[pltpu.VMEM((hist + qb, da), BF16),
                        pltpu.VMEM((hist + qb, da), BF16),
                        pltpu.VMEM((ATT_HEADS, qb, hist + qb), F32)],
        compiler_params=_cparams(("arbitrary", "arbitrary")),
        name="att_mixer",
    )(q, k, v, grow)


def _tail_kernel(x_ref, yc_ref, yr_ref, ya_ref, wo_ref, g1_ref, l1g_ref, l1b_ref,
                 sc_ref, sh_ref, g2_ref, w1_ref, w2_ref, l2g_ref, l2b_ref, o_ref, a_s, x1_s, h_s,
                 *, alpha, d_ff, fc):
    y = jnp.concatenate([yc_ref[0], yr_ref[0], ya_ref[0]], axis=-1)
    y = jnp.dot(y, wo_ref[...], preferred_element_type=F32)
    x1 = _layer_norm(alpha * x_ref[0] + (1.0 + g1_ref[0]) * y, l1g_ref[...], l1b_ref[...])
    x1_s[...] = x1
    h_s[...] = (x1 * (1.0 + sc_ref[0]) + sh_ref[0]).astype(BF16)
    for c in range(d_ff // fc):
        gt = jnp.dot(h_s[...], w1_ref[:, c * fc:(c + 1) * fc], preferred_element_type=F32)
        up = jnp.dot(h_s[...], w1_ref[:, d_ff + c * fc:d_ff + (c + 1) * fc],
                     preferred_element_type=F32)
        a_s[:, c * fc:(c + 1) * fc] = (_silu(gt) * up).astype(BF16)
    y2 = jnp.dot(a_s[...], w2_ref[...], preferred_element_type=F32)
    o_ref[0] = _layer_norm(alpha * x1_s[...] + (1.0 + g2_ref[0]) * y2, l2g_ref[...], l2b_ref[...])


def _tail(x, yc, yr, ya, wo_bf16, g1, ln1_g, ln1_b, sc2, sh2, g2, w1_bf16, w2_bf16, ln2_g, ln2_b,
          alpha, tm):
    b, t, d = x.shape
    d_ff = w2_bf16.shape[0]
    tile = lambda a: pl.BlockSpec((1, tm, a.shape[2]), lambda i, j: (i, j, 0))
    mod = pl.BlockSpec((1, 1, d), lambda i, j: (i, 0, 0))
    row = pl.BlockSpec((1, d), lambda i, j: (0, 0))
    weight = lambda a: pl.BlockSpec(a.shape, lambda i, j: (0, 0), pipeline_mode=pl.Buffered(1))
    return pl.pallas_call(
        functools.partial(_tail_kernel, alpha=alpha, d_ff=d_ff, fc=FFN_FC),
        out_shape=jax.ShapeDtypeStruct((b, t, d), F32),
        grid=(b, t // tm),
        in_specs=[tile(x), tile(yc), tile(yr), tile(ya), weight(wo_bf16), mod, row, row,
                  mod, mod, mod, weight(w1_bf16), weight(w2_bf16), row, row],
        out_specs=tile(x),
        scratch_shapes=[pltpu.VMEM((tm, d_ff), BF16),
                        pltpu.VMEM((tm, d), F32),
                        pltpu.VMEM((tm, d), BF16)],
        compiler_params=_cparams(("arbitrary", "arbitrary")),
        name="tail",
    )(x, yc, yr, ya, wo_bf16, g1, ln1_g.reshape(1, d), ln1_b.reshape(1, d), sc2, sh2, g2,
      w1_bf16, w2_bf16, ln2_g.reshape(1, d), ln2_b.reshape(1, d))


def kernel(x, c, w_ada, b_ada, w_in, conv_w, conv_b, conv_ln_g, conv_ln_b, rec_lower_bound,
           rec_norm_g, rel_bias, w_out, ln1_g, ln1_b, ln2_g, ln2_b, w_ffn_in, w_ffn_out):
    depth = w_in.shape[0]
    b, t, d = x.shape
    alpha = (2 * depth) ** 0.25

    lbs = jax.nn.softmax(rec_lower_bound.astype(F32), axis=0)
    lbs = jnp.cumsum(lbs, axis=0) - lbs[0]
    mods = _ada_mod(c, w_ada, b_ada)

    for l in range(depth):
        sh1, sc1, g1, sh2, sc2, g2 = [m.reshape(b, 1, d) for m in jnp.split(mods[l], 6, axis=-1)]
        y_conv, rq, lf, rk, ri, rg, aq, ak, av = _front(
            x, sc1, sh1, w_in[l].astype(BF16), conv_w[l], conv_b[l], conv_ln_g[l], conv_ln_b[l],
            lbs[l], tm=512)
        y_rec = _rec_mixer(rq, lf, rk, ri, rg, rec_norm_g[l], tb=256)
        y_att = _att_mixer(aq, ak, av, rel_bias[l], qb=ATT_QB)
        x = _tail(x, y_conv, y_rec, y_att, w_out[l].astype(BF16), g1, ln1_g[l], ln1_b[l],
                  sc2, sh2, g2, w_ffn_in[l].astype(BF16), w_ffn_out[l].astype(BF16),
                  ln2_g[l], ln2_b[l], alpha, tm=512)
    return x
```

```python
import functools
import math

import numpy as np
import jax
import jax.numpy as jnp
from jax import lax
from jax.experimental import pallas as pl
from jax.experimental.pallas import tpu as pltpu

F32 = jnp.float32
BF16 = jnp.bfloat16

CHUNK = 64
CONV_WIDTH = 31
REC_HEADS = 4
REC_D = 128
ATT_HEADS = 4
ATT_DH = 64
ATT_LEFT = 8
MAX_REL = 128
LN_EPS = 1e-5
NEG_BIG = -1e30
TINY = 1e-30
LOG2E = math.log2(math.e)

TILE_ROWS = 512
REC_TB = 512
ATT_QB = 256
CONV_TAIL = 32
CONV_RS = 32
GATE_RS = 16
FFN_FC = 256
SUBLANES = 8

VMEM_LIMIT = 56 * 1024 * 1024


def _cparams(sem):
    return pltpu.CompilerParams(dimension_semantics=sem, vmem_limit_bytes=VMEM_LIMIT)


def _layer_norm(x, g, b):
    mu = jnp.mean(x, axis=-1, keepdims=True)
    xc = x - mu
    var = jnp.mean(xc * xc, axis=-1, keepdims=True)
    return xc * lax.rsqrt(var + LN_EPS) * g + b


def _silu(x):
    return x * jax.nn.sigmoid(x)


def _ada_kernel(c_ref, w_ref, b_ref, o_ref):
    ca = _silu(c_ref[...]).astype(BF16)
    o_ref[0] = jnp.dot(ca, w_ref[0].astype(BF16), preferred_element_type=F32) + b_ref[0]


def _ada_mod(c, w_ada, b_ada):
    depth, d, n6 = w_ada.shape
    b = c.shape[0]
    bp = ((b + 7) // 8) * 8
    cp = jnp.zeros((bp, d), F32).at[:b].set(c)
    tn = 1536
    out = pl.pallas_call(
        _ada_kernel,
        out_shape=jax.ShapeDtypeStruct((depth, bp, n6), F32),
        grid=(depth, n6 // tn),
        in_specs=[pl.BlockSpec((bp, d), lambda l, j: (0, 0)),
                  pl.BlockSpec((1, d, tn), lambda l, j: (l, 0, j)),
                  pl.BlockSpec((1, 1, tn), lambda l, j: (l, 0, j))],
        out_specs=pl.BlockSpec((1, bp, tn), lambda l, j: (l, 0, j)),
        compiler_params=_cparams(("arbitrary", "arbitrary")),
        name="ada_mod",
    )(cp, w_ada, b_ada.reshape(depth, 1, n6))
    return out[:, :b]


def _front_kernel(x_ref, sc_ref, sh_ref, w_ref, cw_ref, cb_ref, cg_ref, cbe_ref, lb_ref,
                  yc_ref, rq_ref, lf_ref, rk_ref, ri_ref, rg_ref, aq_ref, ak_ref, av_ref,
                  shs, h_s, z_s):
    tm = x_ref.shape[1]
    dc = cw_ref.shape[1]
    dr = lb_ref.shape[1]
    da = aq_ref.shape[2]
    rec0 = 2 * dc
    att0 = 2 * dc + 4 * dr
    quarter = tm // 4

    def stage(fn):
        fn()

    def proj(c0, n):
        return jnp.dot(h_s[...], w_ref[:, c0:c0 + n], preferred_element_type=F32)

    def conv_rows(lo, hi):
        off = CONV_TAIL - (CONV_WIDTH - 1)
        for r0 in range(lo, hi, CONV_RS):
            acc = jnp.zeros((CONV_RS, dc), F32)
            for j in range(CONV_WIDTH):
                a, s = divmod(off + j, SUBLANES)
                acc = acc + cw_ref[j:j + 1, :] * shs[s, r0 + a * SUBLANES:r0 + a * SUBLANES + CONV_RS, :]
            y = _layer_norm(acc + cb_ref[...], cg_ref[...], cbe_ref[...])
            yc_ref[0, r0:r0 + CONV_RS, :] = _silu(y).astype(BF16)

    def gate_rows(lo, hi):
        lb = lb_ref[...]
        lb_floor = jnp.maximum(lb, TINY)
        one_m_lb = 1.0 - lb
        for r0 in range(lo, hi, GATE_RS):
            z = z_s[r0:r0 + GATE_RS, :]
            e = jnp.exp(-jnp.abs(z))
            r = 1.0 / (1.0 + e)
            pos = z >= 0.0
            lf_ref[0, r0:r0 + GATE_RS, :] = jnp.log(lb_floor + one_m_lb * (jnp.where(pos, 1.0, e) * r))
            rk_ref[0, r0:r0 + GATE_RS, :] = one_m_lb * (jnp.where(pos, e, 1.0) * r)

    @pl.when(pl.program_id(1) == 0)
    def _():
        shs[0, 0:CONV_TAIL, :] = jnp.zeros((CONV_TAIL, dc), F32)

    @pl.when(pl.program_id(1) > 0)
    def _():
        shs[0, 0:CONV_TAIL, :] = shs[0, tm:tm + CONV_TAIL, :]

    h_s[...] = (x_ref[0] * (1.0 + sc_ref[0]) + sh_ref[0]).astype(BF16)

    @stage
    def _():
        pc = proj(0, 2 * dc)
        for r0 in range(0, tm, CONV_RS):
            blk = pc[r0:r0 + CONV_RS, :]
            shs[0, CONV_TAIL + r0:CONV_TAIL + r0 + CONV_RS, :] = (
                blk[:, :dc] * jax.nn.sigmoid(blk[:, dc:]))

    @stage
    def _():
        rq_ref[0] = proj(rec0, dr)
        n_sh = tm + CONV_TAIL - SUBLANES
        for s in range(1, SUBLANES):
            for r0 in range(0, n_sh, CONV_RS):
                n = min(CONV_RS, n_sh - r0)
                shs[s, r0:r0 + n, :] = shs[0, r0 + s:r0 + s + n, :]

    @stage
    def _():
        z_s[...] = proj(rec0 + dr, dr)
        conv_rows(0, quarter)

    @stage
    def _():
        ri_ref[0] = proj(rec0 + 2 * dr, dr).astype(BF16)
        conv_rows(quarter, 2 * quarter)
        gate_rows(0, 2 * quarter)

    @stage
    def _():
        rg_ref[0] = proj(rec0 + 3 * dr, dr)
        conv_rows(2 * quarter, 3 * quarter)
        gate_rows(2 * quarter, tm)

    @stage
    def _():
        aq_ref[0] = (proj(att0, da) * (LOG2E * ATT_DH ** -0.5)).astype(BF16)
        ak_ref[0] = proj(att0 + da, da).astype(BF16)
        av_ref[0] = proj(att0 + 2 * da, da).astype(BF16)
        conv_rows(3 * quarter, tm)


def _layer_block(stacked, layer, **kw):
    return pl.BlockSpec((None,) + stacked.shape[1:], lambda i, j: (layer, 0, 0), **kw)


def _front(x, sc, sh, w_all_bf16, layer, conv_w, conv_b, conv_ln_g, conv_ln_b, lb, tm):
    b, t, d = x.shape
    dc = conv_w.shape[1]
    dr = lb.shape[0]
    da = ATT_HEADS * ATT_DH
    row = lambda a: a.reshape(1, -1)
    const = lambda a: pl.BlockSpec(a.shape, lambda i, j: (0,) * a.ndim)
    mod = pl.BlockSpec((1, 1, d), lambda i, j: (i, 0, 0))
    tile = lambda n: pl.BlockSpec((1, tm, n), lambda i, j: (i, j, 0))
    outs = [(dc, BF16), (dr, F32), (dr, F32), (dr, F32), (dr, BF16), (dr, F32),
            (da, BF16), (da, BF16), (da, BF16)]
    consts = [conv_w, row(conv_b), row(conv_ln_g), row(conv_ln_b), row(lb)]
    return pl.pallas_call(
        _front_kernel,
        out_shape=[jax.ShapeDtypeStruct((b, t, n), dt) for n, dt in outs],
        grid=(b, t // tm),
        in_specs=[tile(d), mod, mod, _layer_block(w_all_bf16, layer)] + [const(a) for a in consts],
        out_specs=[tile(n) for n, _ in outs],
        scratch_shapes=[pltpu.VMEM((SUBLANES, tm + CONV_TAIL, dc), F32),
                        pltpu.VMEM((tm, d), BF16),
                        pltpu.VMEM((tm, dr), F32)],
        compiler_params=_cparams(("arbitrary", "arbitrary")),
        name="front",
    )(x, sc, sh, w_all_bf16, *consts)


REC_LEVELS = tuple(CHUNK >> (i + 1) for i in range(int(math.log2(CHUNK))))
REC_SAFE_RANGE = 150.0


def _rec_tables():
    x = np.arange(CHUNK)
    sel = [x[:, None] >= x[None, :]]
    masks = []
    for m in REC_LEVELS:
        bnd = (x // (2 * m)) * (2 * m) + m - 1
        sel.append(bnd[:, None] >= x[None, :])
        same = (x[:, None] // (2 * m)) == (x[None, :] // (2 * m))
        masks.append(same & ((x[:, None] & m) != 0) & ((x[None, :] & m) == 0))
    masks.append(x[:, None] == x[None, :])
    sel = np.concatenate(sel, axis=0).astype(np.float32)
    return np.concatenate([sel, sel, sel], axis=1), np.stack(masks).astype(np.float32)


def _split3(x):
    hi = x.astype(BF16)
    r1 = x - hi.astype(F32)
    mid = r1.astype(BF16)
    lo = (r1 - mid.astype(F32)).astype(BF16)
    return hi, mid, lo


def _rec_kernel(q_ref, lf_ref, k_ref, i_ref, g_ref, ng_ref, sel_ref, pm_ref, o_ref, st_ref):
    tb = q_ref.shape[1]
    dd = q_ref.shape[2]
    nh = dd // REC_D
    nchunks = tb // CHUNK
    nt = (((1,), (1,)), ((), ()))
    tn = (((0,), (0,)), ((), ()))

    @pl.when(pl.program_id(1) == 0)
    def _():
        st_ref[...] = jnp.zeros(st_ref.shape, F32)

    tot = [jnp.sum(lf_ref[0, c * CHUNK:(c + 1) * CHUNK, :], axis=0, keepdims=True)
           for c in range(nchunks)]
    safe = jnp.min(functools.reduce(jnp.minimum, tot)) >= -REC_SAFE_RANGE

    def prefix(lf, nrows):
        return jnp.dot(sel_ref[0:nrows, :], jnp.concatenate(_split3(lf), axis=0),
                       preferred_element_type=F32)

    def head_out(h, a, qg, kg, vb, eb):
        sl = slice(h * REC_D, (h + 1) * REC_D)
        st = st_ref[h]
        o_h = (lax.dot_general(qg[:, sl], st.astype(BF16), nt, preferred_element_type=F32)
               + jnp.dot(a.astype(BF16), vb[:, sl], preferred_element_type=F32))
        upd = lax.dot_general(vb[:, sl], kg[:, sl], tn, preferred_element_type=F32)
        st_ref[h] = st * eb[:, sl] + upd
        ms = jnp.mean(o_h * o_h, axis=-1, keepdims=True)
        return o_h * lax.rsqrt(ms + LN_EPS) * ng_ref[...]

    def store(rs, outs):
        y = jnp.concatenate(outs, axis=-1) * _silu(g_ref[0, rs, :])
        o_ref[0, rs, :] = y.astype(o_ref.dtype)

    @pl.when(safe)
    def _():
        causal = (lax.broadcasted_iota(jnp.int32, (CHUNK, CHUNK), 0)
                  >= lax.broadcasted_iota(jnp.int32, (CHUNK, CHUNK), 1))
        heads = [slice(h * REC_D, (h + 1) * REC_D) for h in range(nh)]
        rows = [slice(c * CHUNK, (c + 1) * CHUNK) for c in range(nchunks)]
        bcs = [prefix(lf_ref[0, rs, :], CHUNK) for rs in rows]
        qg, kg, vb, eb, att, upd = [], [], [], [], [], []
        for rs, bc in zip(rows, bcs):
            half = 0.5 * bc[CHUNK - 1:CHUNK, :]
            eh = jnp.exp(half)
            qf = q_ref[0, rs, :] * jnp.exp(bc - half)
            kf = k_ref[0, rs, :] * jnp.exp(half - bc)
            qi = qf.astype(BF16)
            ki = kf.astype(BF16)
            qg.append((qf * eh).astype(BF16))
            kg.append((kf * eh).astype(BF16))
            vb.append(i_ref[0, rs, :])
            eb.append(eh * eh)
            att.append([jnp.where(causal, lax.dot_general(qi[:, sl], ki[:, sl], nt,
                                                          preferred_element_type=F32),
                                  0.0).astype(BF16) for sl in heads])
        for c in range(nchunks):
            upd.append([lax.dot_general(vb[c][:, sl], kg[c][:, sl], tn, preferred_element_type=F32)
                        for sl in heads])
        states = []
        for h, sl in enumerate(heads):
            st = st_ref[h]
            per_chunk = []
            for c in range(nchunks):
                per_chunk.append(st.astype(BF16))
                st = st * eb[c][:, sl] + upd[c][h]
            st_ref[h] = st
            states.append(per_chunk)
        for c, rs in enumerate(rows):
            outs = []
            for h, sl in enumerate(heads):
                o_h = (lax.dot_general(qg[c][:, sl], states[h][c], nt, preferred_element_type=F32)
                       + jnp.dot(att[c][h], vb[c][:, sl], preferred_element_type=F32))
                ms = jnp.mean(o_h * o_h, axis=-1, keepdims=True)
                outs.append(o_h * lax.rsqrt(ms + LN_EPS) * ng_ref[...])
            store(rs, outs)

    @pl.when(jnp.logical_not(safe))
    def _():
        row_id = lax.broadcasted_iota(jnp.int32, (CHUNK, dd), 0)

        def chunk_body(ci, carry):
            rs = pl.ds(pl.multiple_of(ci * CHUNK, CHUNK), CHUNK)
            q = q_ref[0, rs, :]
            k = k_ref[0, rs, :]
            pre = prefix(lf_ref[0, rs, :], sel_ref.shape[0])
            bc = pre[0:CHUNK, :]
            bl = bc[CHUNK - 1:CHUNK, :]
            qg = (q * jnp.exp(bc)).astype(BF16)
            kg = (k * jnp.exp(bl - bc)).astype(BF16)
            vb = i_ref[0, rs, :]
            xs = []
            for li, m in enumerate(REC_LEVELS):
                dec = jnp.exp(-jnp.abs(bc - pre[(li + 1) * CHUNK:(li + 2) * CHUNK, :]))
                xs.append((jnp.where((row_id & m) != 0, q, k) * dec).astype(BF16))
            qb16 = q.astype(BF16)
            kb16 = k.astype(BF16)
            outs = []
            for h in range(nh):
                sl = slice(h * REC_D, (h + 1) * REC_D)
                a = (lax.dot_general(qb16[:, sl], kb16[:, sl], nt, preferred_element_type=F32)
                     * pm_ref[len(REC_LEVELS)])
                for li in range(len(REC_LEVELS)):
                    xh = xs[li][:, sl]
                    a = a + lax.dot_general(xh, xh, nt, preferred_element_type=F32) * pm_ref[li]
                outs.append(head_out(h, a, qg, kg, vb, jnp.exp(bl)))
            store(rs, outs)
            return carry

        lax.fori_loop(0, nchunks, chunk_body, 0)


def _rec_mixer(q, lf, k, v, g, norm_g, tb):
    b, t, dd = q.shape
    nh = dd // REC_D
    sel, pmask = _rec_tables()
    sel = jnp.asarray(sel, BF16)
    pmask = jnp.asarray(pmask, F32)
    tile = pl.BlockSpec((1, tb, dd), lambda i, j: (i, j, 0))
    return pl.pallas_call(
        _rec_kernel,
        out_shape=jax.ShapeDtypeStruct((b, t, dd), BF16),
        grid=(b, t // tb),
        in_specs=[tile, tile, tile, tile, tile,
                  pl.BlockSpec((1, REC_D), lambda i, j: (0, 0)),
                  pl.BlockSpec(sel.shape, lambda i, j: (0, 0)),
                  pl.BlockSpec(pmask.shape, lambda i, j: (0, 0, 0))],
        out_specs=tile,
        scratch_shapes=[pltpu.VMEM((nh, REC_D, REC_D), F32)],
        compiler_params=_cparams(("arbitrary", "arbitrary")),
        name="rec_mixer",
    )(q, lf, k, v, g, norm_g.reshape(1, REC_D), sel, pmask)


def _att_kernel(q_ref, k_ref, v_ref, g_ref, o_ref, kbuf, vbuf, bias_ref):
    qb = q_ref.shape[1]
    da = q_ref.shape[2]
    hist = kbuf.shape[0] - qb
    t = pl.program_id(1)

    @pl.when((pl.program_id(0) == 0) & (t == 0))
    def _():
        gw = g_ref.shape[2]
        qrow = lax.broadcasted_iota(jnp.int32, (qb, hist + qb), 0)
        kcol = lax.broadcasted_iota(jnp.int32, (qb, hist + qb), 1)
        dchunk = kcol // CHUNK - qrow // CHUNK
        band = (dchunk >= 0) & (dchunk <= ATT_LEFT)
        for h in range(da // ATT_DH):
            gb = jnp.broadcast_to(g_ref[h], (qb, gw))
            skew = pltpu.roll(gb, gw - (qb - 1), 1, stride=1, stride_axis=0)
            bias_ref[h] = jnp.where(band, skew[:, :hist + qb] * LOG2E, NEG_BIG)

    @pl.when(t == 0)
    def _():
        kbuf[0:hist, :] = jnp.zeros((hist, da), BF16)
        vbuf[0:hist, :] = jnp.zeros((hist, da), BF16)

    @pl.when(t > 0)
    def _():
        kbuf[0:hist, :] = kbuf[qb:qb + hist, :]
        vbuf[0:hist, :] = vbuf[qb:qb + hist, :]

    kbuf[hist:hist + qb, :] = k_ref[0]
    vbuf[hist:hist + qb, :] = v_ref[0]

    nh = da // ATT_DH
    lane_head = lax.broadcasted_iota(jnp.int32, (qb, da), 1) // ATT_DH

    def attend(mask_start):
        q = q_ref[0]
        kk = kbuf[...]
        vv = vbuf[...]
        scores = [lax.dot_general(jnp.where(lane_head == h, q, jnp.zeros_like(q)), kk,
                                  (((1,), (1,)), ((), ())), preferred_element_type=F32)
                  for h in range(nh)]
        if mask_start:
            kidx = lax.broadcasted_iota(jnp.int32, (qb, hist + qb), 1)
            in_seq = kidx >= hist - t * qb
        out = jnp.zeros((qb, da), F32)
        for h in range(nh):
            s = scores[h] + bias_ref[h]
            if mask_start:
                s = jnp.where(in_seq, s, NEG_BIG)
            e = jnp.exp2(s - jnp.max(s, axis=-1, keepdims=True))
            l = jnp.sum(e, axis=-1, keepdims=True)
            oh = jnp.dot(e.astype(BF16), vv, preferred_element_type=F32)
            out = out + jnp.where(lane_head == h, oh / l, 0.0)
        o_ref[0] = out.astype(o_ref.dtype)

    @pl.when(t * qb < hist)
    def _():
        attend(True)

    @pl.when(t * qb >= hist)
    def _():
        attend(False)


def _att_rel_row(rel_table, qb):
    hist = ATT_LEFT * CHUNK
    gw = pl.next_power_of_2(2 * qb + hist - 1)
    n_lo = qb - 1 + hist - MAX_REL
    n_hi = gw - n_lo - (2 * MAX_REL + 1)
    nh = rel_table.shape[0]
    tab = rel_table.astype(F32)
    row = jnp.concatenate([jnp.broadcast_to(tab[:, :1], (nh, n_lo)), tab,
                           jnp.broadcast_to(tab[:, -1:], (nh, n_hi))], axis=1)
    return row.reshape(nh, 1, gw)


def _att_mixer(q, k, v, rel_table, qb):
    b, t, da = q.shape
    hist = ATT_LEFT * CHUNK
    grow = _att_rel_row(rel_table, qb)
    tile = pl.BlockSpec((1, qb, da), lambda i, j: (i, j, 0))
    return pl.pallas_call(
        _att_kernel,
        out_shape=jax.ShapeDtypeStruct((b, t, da), BF16),
        grid=(b, t // qb),
        in_specs=[tile, tile, tile, pl.BlockSpec(grow.shape, lambda i, j: (0, 0, 0))],
        out_specs=tile,
        scratch_shapes=[pltpu.VMEM((hist + qb, da), BF16),
                        pltpu.VMEM((hist + qb, da), BF16),
                        pltpu.VMEM((ATT_HEADS, qb, hist + qb), F32)],
        compiler_params=_cparams(("arbitrary", "arbitrary")),
        name="att_mixer",
    )(q, k, v, grow)


def _tail_kernel(x_ref, yc_ref, yr_ref, ya_ref, wo_ref, g1_ref, l1g_ref, l1b_ref,
                 sc_ref, sh_ref, g2_ref, w1_ref, w2_ref, l2g_ref, l2b_ref, o_ref, a_s, x1_s, h_s,
                 *, alpha, d_ff, fc):
    y = jnp.concatenate([yc_ref[0], yr_ref[0], ya_ref[0]], axis=-1)
    y = jnp.dot(y, wo_ref[...], preferred_element_type=F32)
    x1 = _layer_norm(alpha * x_ref[0] + (1.0 + g1_ref[0]) * y, l1g_ref[...], l1b_ref[...])
    x1_s[...] = x1
    h_s[...] = (x1 * (1.0 + sc_ref[0]) + sh_ref[0]).astype(BF16)
    for c in range(d_ff // fc):
        gt = jnp.dot(h_s[...], w1_ref[:, c * fc:(c + 1) * fc], preferred_element_type=F32)
        up = jnp.dot(h_s[...], w1_ref[:, d_ff + c * fc:d_ff + (c + 1) * fc],
                     preferred_element_type=F32)
        a_s[:, c * fc:(c + 1) * fc] = (_silu(gt) * up).astype(BF16)
    y2 = jnp.dot(a_s[...], w2_ref[...], preferred_element_type=F32)
    o_ref[0] = _layer_norm(alpha * x1_s[...] + (1.0 + g2_ref[0]) * y2, l2g_ref[...], l2b_ref[...])


def _tail(x, yc, yr, ya, wo_bf16, g1, ln1_g, ln1_b, sc2, sh2, g2, w1_bf16, w2_bf16, ln2_g, ln2_b,
          layer, alpha, tm):
    b, t, d = x.shape
    d_ff = w2_bf16.shape[1]
    tile = lambda a: pl.BlockSpec((1, tm, a.shape[2]), lambda i, j: (i, j, 0))
    mod = pl.BlockSpec((1, 1, d), lambda i, j: (i, 0, 0))
    row = pl.BlockSpec((1, d), lambda i, j: (0, 0))
    weight = lambda a: _layer_block(a, layer, pipeline_mode=pl.Buffered(1))
    return pl.pallas_call(
        functools.partial(_tail_kernel, alpha=alpha, d_ff=d_ff, fc=FFN_FC),
        out_shape=jax.ShapeDtypeStruct((b, t, d), F32),
        grid=(b, t // tm),
        in_specs=[tile(x), tile(yc), tile(yr), tile(ya), weight(wo_bf16), mod, row, row,
                  mod, mod, mod, weight(w1_bf16), weight(w2_bf16), row, row],
        out_specs=tile(x),
        scratch_shapes=[pltpu.VMEM((tm, d_ff), BF16),
                        pltpu.VMEM((tm, d), F32),
                        pltpu.VMEM((tm, d), BF16)],
        compiler_params=_cparams(("arbitrary", "arbitrary")),
        name="tail",
    )(x, yc, yr, ya, wo_bf16, g1, ln1_g.reshape(1, d), ln1_b.reshape(1, d), sc2, sh2, g2,
      w1_bf16, w2_bf16, ln2_g.reshape(1, d), ln2_b.reshape(1, d))


def kernel(x, c, w_ada, b_ada, w_in, conv_w, conv_b, conv_ln_g, conv_ln_b, rec_lower_bound,
           rec_norm_g, rel_bias, w_out, ln1_g, ln1_b, ln2_g, ln2_b, w_ffn_in, w_ffn_out):
    depth = w_in.shape[0]
    b, t, d = x.shape
    alpha = (2 * depth) ** 0.25

    lbs = jax.nn.softmax(rec_lower_bound.astype(F32), axis=0)
    lbs = jnp.cumsum(lbs, axis=0) - lbs[0]
    mods = _ada_mod(c, w_ada, b_ada)
    w_in, w_out, w_ffn_in, w_ffn_out = [w.astype(BF16) for w in (w_in, w_out, w_ffn_in, w_ffn_out)]

    for l in range(depth):
        sh1, sc1, g1, sh2, sc2, g2 = [m.reshape(b, 1, d) for m in jnp.split(mods[l], 6, axis=-1)]
        y_conv, rq, lf, rk, ri, rg, aq, ak, av = _front(
            x, sc1, sh1, w_in, l, conv_w[l], conv_b[l], conv_ln_g[l], conv_ln_b[l], lbs[l],
            tm=TILE_ROWS)
        y_rec = _rec_mixer(rq, lf, rk, ri, rg, rec_norm_g[l], tb=REC_TB)
        y_att = _att_mixer(aq, ak, av, rel_bias[l], qb=ATT_QB)
        x = _tail(x, y_conv, y_rec, y_att, w_out, g1, ln1_g[l], ln1_b[l], sc2, sh2, g2,
                  w_ffn_in, w_ffn_out, ln2_g[l], ln2_b[l], l, alpha, tm=TILE_ROWS)
    return x
```

```python
import functools
import math

import numpy as np
import jax
import jax.numpy as jnp
from jax import lax
from jax.experimental import pallas as pl
from jax.experimental.pallas import tpu as pltpu

F32 = jnp.float32
BF16 = jnp.bfloat16

CHUNK = 64
CONV_WIDTH = 31
REC_HEADS = 4
REC_D = 128
ATT_HEADS = 4
ATT_DH = 64
ATT_LEFT = 8
MAX_REL = 128
LN_EPS = 1e-5
NEG_BIG = -1e30
TINY = 1e-30
LOG2E = math.log2(math.e)

FRONT_ROWS = 512
TILE_ROWS = 512
REC_TB = 512
ATT_QB = 256
CONV_TAIL = 32
CONV_RS = 32
GATE_RS = 16
FFN_FC = 256
TAIL_SPLIT = 2
SUBLANES = 8

VMEM_LIMIT = 56 * 1024 * 1024


def _cparams(sem):
    return pltpu.CompilerParams(dimension_semantics=sem, vmem_limit_bytes=VMEM_LIMIT)


def _layer_norm(x, g, b):
    mu = jnp.mean(x, axis=-1, keepdims=True)
    xc = x - mu
    var = jnp.mean(xc * xc, axis=-1, keepdims=True)
    return xc * lax.rsqrt(var + LN_EPS) * g + b


def _silu(x):
    return x * jax.nn.sigmoid(x)


def _ada_kernel(c_ref, w_ref, b_ref, o_ref):
    ca = _silu(c_ref[...]).astype(BF16)
    o_ref[0] = jnp.dot(ca, w_ref[0].astype(BF16), preferred_element_type=F32) + b_ref[0]


def _ada_mod(c, w_ada, b_ada):
    depth, d, n6 = w_ada.shape
    b = c.shape[0]
    bp = ((b + 7) // 8) * 8
    cp = jnp.zeros((bp, d), F32).at[:b].set(c)
    tn = 1536
    out = pl.pallas_call(
        _ada_kernel,
        out_shape=jax.ShapeDtypeStruct((depth, bp, n6), F32),
        grid=(depth, n6 // tn),
        in_specs=[pl.BlockSpec((bp, d), lambda l, j: (0, 0)),
                  pl.BlockSpec((1, d, tn), lambda l, j: (l, 0, j)),
                  pl.BlockSpec((1, 1, tn), lambda l, j: (l, 0, j))],
        out_specs=pl.BlockSpec((1, bp, tn), lambda l, j: (l, 0, j)),
        compiler_params=_cparams(("arbitrary", "arbitrary")),
        name="ada_mod",
    )(cp, w_ada, b_ada.reshape(depth, 1, n6))
    return out[:, :b]


def _front_kernel(x_ref, sc_ref, sh_ref, w_ref, cw_ref, cb_ref, cg_ref, cbe_ref, lb_ref,
                  yc_ref, rq_ref, lf_ref, rk_ref, ri_ref, rg_ref, aq_ref, ak_ref, av_ref,
                  shs, h_s, z_s):
    tm = x_ref.shape[1]
    dc = cw_ref.shape[1]
    dr = lb_ref.shape[1]
    da = aq_ref.shape[2]
    rec0 = 2 * dc
    att0 = 2 * dc + 4 * dr
    quarter = tm // 4

    def stage(fn):
        fn()

    def proj(c0, n):
        return jnp.dot(h_s[...], w_ref[:, c0:c0 + n], preferred_element_type=F32)

    def paced_zero(dep):
        bits = lax.bitcast_convert_type(dep[0:SUBLANES, 0:128], jnp.uint32)
        zero = lax.shift_right_logical(lax.shift_right_logical(bits, jnp.uint32(16)), jnp.uint32(16))
        zero = lax.bitcast_convert_type(zero, F32)
        return jnp.tile(zero, (CONV_RS // SUBLANES, dc // 128))

    def conv_rows(lo, hi, dep):
        off = CONV_TAIL - (CONV_WIDTH - 1)
        acc = dep
        for r0 in range(lo, hi, CONV_RS):
            acc = paced_zero(acc)
            for j in range(CONV_WIDTH):
                a, s = divmod(off + j, SUBLANES)
                acc = acc + cw_ref[j:j + 1, :] * shs[s, r0 + a * SUBLANES:r0 + a * SUBLANES + CONV_RS, :]
            y = _layer_norm(acc + cb_ref[...], cg_ref[...], cbe_ref[...])
            yc_ref[0, r0:r0 + CONV_RS, :] = _silu(y).astype(BF16)

    def gate_rows(lo, hi):
        lb = lb_ref[...]
        lb_floor = jnp.maximum(lb, TINY)
        one_m_lb = 1.0 - lb
        for r0 in range(lo, hi, GATE_RS):
            z = z_s[r0:r0 + GATE_RS, :]
            e = jnp.exp(-jnp.abs(z))
            r = 1.0 / (1.0 + e)
            pos = z >= 0.0
            lf_ref[0, r0:r0 + GATE_RS, :] = jnp.log(lb_floor + one_m_lb * (jnp.where(pos, 1.0, e) * r))
            rk_ref[0, r0:r0 + GATE_RS, :] = one_m_lb * (jnp.where(pos, e, 1.0) * r)

    @pl.when(pl.program_id(1) == 0)
    def _():
        shs[0, 0:CONV_TAIL, :] = jnp.zeros((CONV_TAIL, dc), F32)

    @pl.when(pl.program_id(1) > 0)
    def _():
        shs[0, 0:CONV_TAIL, :] = shs[0, tm:tm + CONV_TAIL, :]

    h_s[...] = (x_ref[0] * (1.0 + sc_ref[0]) + sh_ref[0]).astype(BF16)

    @stage
    def _():
        pc = proj(0, 2 * dc)
        for r0 in range(0, tm, CONV_RS):
            blk = pc[r0:r0 + CONV_RS, :]
            shs[0, CONV_TAIL + r0:CONV_TAIL + r0 + CONV_RS, :] = (
                blk[:, :dc] * jax.nn.sigmoid(blk[:, dc:]))

    @stage
    def _():
        rq_ref[0] = proj(rec0, dr)
        n_sh = tm + CONV_TAIL - SUBLANES
        for s in range(1, SUBLANES):
            for r0 in range(0, n_sh, CONV_RS):
                n = min(CONV_RS, n_sh - r0)
                shs[s, r0:r0 + n, :] = shs[0, r0 + s:r0 + s + n, :]

    @stage
    def _():
        res = proj(rec0 + dr, dr)
        z_s[...] = res
        conv_rows(0, quarter, res)

    @stage
    def _():
        res = proj(rec0 + 2 * dr, dr)
        ri_ref[0] = res.astype(BF16)
        conv_rows(quarter, 2 * quarter, res)
        gate_rows(0, 2 * quarter)

    @stage
    def _():
        res = proj(rec0 + 3 * dr, dr)
        rg_ref[0] = res
        conv_rows(2 * quarter, 3 * quarter, res)
        gate_rows(2 * quarter, tm)

    @stage
    def _():
        aq_ref[0] = (proj(att0, da) * (LOG2E * ATT_DH ** -0.5)).astype(BF16)
        ak_ref[0] = proj(att0 + da, da).astype(BF16)
        res = proj(att0 + 2 * da, da)
        av_ref[0] = res.astype(BF16)
        conv_rows(3 * quarter, tm, res)


def _layer_block(stacked, layer, **kw):
    return pl.BlockSpec((None,) + stacked.shape[1:], lambda i, j: (layer, 0, 0), **kw)


def _front(x, sc, sh, w_all_bf16, layer, conv_w, conv_b, conv_ln_g, conv_ln_b, lb, tm):
    b, t, d = x.shape
    dc = conv_w.shape[1]
    dr = lb.shape[0]
    da = ATT_HEADS * ATT_DH
    row = lambda a: a.reshape(1, -1)
    const = lambda a: pl.BlockSpec(a.shape, lambda i, j: (0,) * a.ndim)
    mod = pl.BlockSpec((1, 1, d), lambda i, j: (i, 0, 0))
    tile = lambda n: pl.BlockSpec((1, tm, n), lambda i, j: (i, j, 0))
    outs = [(dc, BF16), (dr, F32), (dr, F32), (dr, F32), (dr, BF16), (dr, F32),
            (da, BF16), (da, BF16), (da, BF16)]
    consts = [conv_w, row(conv_b), row(conv_ln_g), row(conv_ln_b), row(lb)]
    return pl.pallas_call(
        _front_kernel,
        out_shape=[jax.ShapeDtypeStruct((b, t, n), dt) for n, dt in outs],
        grid=(b, t // tm),
        in_specs=[tile(d), mod, mod, _layer_block(w_all_bf16, layer)] + [const(a) for a in consts],
        out_specs=[tile(n) for n, _ in outs],
        scratch_shapes=[pltpu.VMEM((SUBLANES, tm + CONV_TAIL, dc), F32),
                        pltpu.VMEM((tm, d), BF16),
                        pltpu.VMEM((tm, dr), F32)],
        compiler_params=_cparams(("arbitrary", "arbitrary")),
        name="front",
    )(x, sc, sh, w_all_bf16, *consts)


REC_LEVELS = tuple(CHUNK >> (i + 1) for i in range(int(math.log2(CHUNK))))
REC_SAFE_RANGE = 150.0


def _rec_tables():
    x = np.arange(CHUNK)
    sel = [x[:, None] >= x[None, :]]
    masks = []
    for m in REC_LEVELS:
        bnd = (x // (2 * m)) * (2 * m) + m - 1
        sel.append(bnd[:, None] >= x[None, :])
        same = (x[:, None] // (2 * m)) == (x[None, :] // (2 * m))
        masks.append(same & ((x[:, None] & m) != 0) & ((x[None, :] & m) == 0))
    masks.append(x[:, None] == x[None, :])
    sel = np.concatenate(sel, axis=0).astype(np.float32)
    return np.concatenate([sel, sel, sel], axis=1), np.stack(masks).astype(np.float32)


def _split3(x):
    hi = x.astype(BF16)
    r1 = x - hi.astype(F32)
    mid = r1.astype(BF16)
    lo = (r1 - mid.astype(F32)).astype(BF16)
    return hi, mid, lo


def _rec_kernel(q_ref, lf_ref, k_ref, i_ref, g_ref, ng_ref, sel_ref, pm_ref, o_ref, st_ref):
    tb = q_ref.shape[1]
    dd = q_ref.shape[2]
    nh = dd // REC_D
    nchunks = tb // CHUNK
    nt = (((1,), (1,)), ((), ()))
    tn = (((0,), (0,)), ((), ()))

    @pl.when(pl.program_id(1) == 0)
    def _():
        st_ref[...] = jnp.zeros(st_ref.shape, F32)

    tot = [jnp.sum(lf_ref[0, c * CHUNK:(c + 1) * CHUNK, :], axis=0, keepdims=True)
           for c in range(nchunks)]
    safe = jnp.min(functools.reduce(jnp.minimum, tot)) >= -REC_SAFE_RANGE

    def prefix(lf, nrows):
        return jnp.dot(sel_ref[0:nrows, :], jnp.concatenate(_split3(lf), axis=0),
                       preferred_element_type=F32)

    def head_out(h, a, qg, kg, vb, eb):
        sl = slice(h * REC_D, (h + 1) * REC_D)
        st = st_ref[h]
        o_h = (lax.dot_general(qg[:, sl], st.astype(BF16), nt, preferred_element_type=F32)
               + jnp.dot(a.astype(BF16), vb[:, sl], preferred_element_type=F32))
        upd = lax.dot_general(vb[:, sl], kg[:, sl], tn, preferred_element_type=F32)
        st_ref[h] = st * eb[:, sl] + upd
        ms = jnp.mean(o_h * o_h, axis=-1, keepdims=True)
        return o_h * lax.rsqrt(ms + LN_EPS) * ng_ref[...]

    def store(rs, outs):
        y = jnp.concatenate(outs, axis=-1) * _silu(g_ref[0, rs, :])
        o_ref[0, rs, :] = y.astype(o_ref.dtype)

    @pl.when(safe)
    def _():
        causal = (lax.broadcasted_iota(jnp.int32, (CHUNK, CHUNK), 0)
                  >= lax.broadcasted_iota(jnp.int32, (CHUNK, CHUNK), 1))
        heads = [slice(h * REC_D, (h + 1) * REC_D) for h in range(nh)]
        rows = [slice(c * CHUNK, (c + 1) * CHUNK) for c in range(nchunks)]
        bcs = [prefix(lf_ref[0, rs, :], CHUNK) for rs in rows]
        qg, kg, vb, eb, att, upd = [], [], [], [], [], []
        for rs, bc in zip(rows, bcs):
            half = 0.5 * bc[CHUNK - 1:CHUNK, :]
            eh = jnp.exp(half)
            qf = q_ref[0, rs, :] * jnp.exp(bc - half)
            kf = k_ref[0, rs, :] * jnp.exp(half - bc)
            qi = qf.astype(BF16)
            ki = kf.astype(BF16)
            qg.append((qf * eh).astype(BF16))
            kg.append((kf * eh).astype(BF16))
            vb.append(i_ref[0, rs, :])
            eb.append(eh * eh)
            att.append([jnp.where(causal, lax.dot_general(qi[:, sl], ki[:, sl], nt,
                                                          preferred_element_type=F32),
                                  0.0).astype(BF16) for sl in heads])
        for c in range(nchunks):
            upd.append([lax.dot_general(vb[c][:, sl], kg[c][:, sl], tn, preferred_element_type=F32)
                        for sl in heads])
        states = []
        for h, sl in enumerate(heads):
            st = st_ref[h]
            per_chunk = []
            for c in range(nchunks):
                per_chunk.append(st.astype(BF16))
                st = st * eb[c][:, sl] + upd[c][h]
            st_ref[h] = st
            states.append(per_chunk)
        for c, rs in enumerate(rows):
            outs = []
            for h, sl in enumerate(heads):
                o_h = (lax.dot_general(qg[c][:, sl], states[h][c], nt, preferred_element_type=F32)
                       + jnp.dot(att[c][h], vb[c][:, sl], preferred_element_type=F32))
                ms = jnp.mean(o_h * o_h, axis=-1, keepdims=True)
                outs.append(o_h * lax.rsqrt(ms + LN_EPS) * ng_ref[...])
            store(rs, outs)

    @pl.when(jnp.logical_not(safe))
    def _():
        row_id = lax.broadcasted_iota(jnp.int32, (CHUNK, dd), 0)

        def chunk_body(ci, carry):
            rs = pl.ds(pl.multiple_of(ci * CHUNK, CHUNK), CHUNK)
            q = q_ref[0, rs, :]
            k = k_ref[0, rs, :]
            pre = prefix(lf_ref[0, rs, :], sel_ref.shape[0])
            bc = pre[0:CHUNK, :]
            bl = bc[CHUNK - 1:CHUNK, :]
            qg = (q * jnp.exp(bc)).astype(BF16)
            kg = (k * jnp.exp(bl - bc)).astype(BF16)
            vb = i_ref[0, rs, :]
            xs = []
            for li, m in enumerate(REC_LEVELS):
                dec = jnp.exp(-jnp.abs(bc - pre[(li + 1) * CHUNK:(li + 2) * CHUNK, :]))
                xs.append((jnp.where((row_id & m) != 0, q, k) * dec).astype(BF16))
            qb16 = q.astype(BF16)
            kb16 = k.astype(BF16)
            outs = []
            for h in range(nh):
                sl = slice(h * REC_D, (h + 1) * REC_D)
                a = (lax.dot_general(qb16[:, sl], kb16[:, sl], nt, preferred_element_type=F32)
                     * pm_ref[len(REC_LEVELS)])
                for li in range(len(REC_LEVELS)):
                    xh = xs[li][:, sl]
                    a = a + lax.dot_general(xh, xh, nt, preferred_element_type=F32) * pm_ref[li]
                outs.append(head_out(h, a, qg, kg, vb, jnp.exp(bl)))
            store(rs, outs)
            return carry

        lax.fori_loop(0, nchunks, chunk_body, 0)


def _rec_mixer(q, lf, k, v, g, norm_g, tb):
    b, t, dd = q.shape
    nh = dd // REC_D
    sel, pmask = _rec_tables()
    sel = jnp.asarray(sel, BF16)
    pmask = jnp.asarray(pmask, F32)
    tile = pl.BlockSpec((1, tb, dd), lambda i, j: (i, j, 0))
    return pl.pallas_call(
        _rec_kernel,
        out_shape=jax.ShapeDtypeStruct((b, t, dd), BF16),
        grid=(b, t // tb),
        in_specs=[tile, tile, tile, tile, tile,
                  pl.BlockSpec((1, REC_D), lambda i, j: (0, 0)),
                  pl.BlockSpec(sel.shape, lambda i, j: (0, 0)),
                  pl.BlockSpec(pmask.shape, lambda i, j: (0, 0, 0))],
        out_specs=tile,
        scratch_shapes=[pltpu.VMEM((nh, REC_D, REC_D), F32)],
        compiler_params=_cparams(("arbitrary", "arbitrary")),
        name="rec_mixer",
    )(q, lf, k, v, g, norm_g.reshape(1, REC_D), sel, pmask)


def _att_kernel(q_ref, k_ref, v_ref, g_ref, o_ref, kbuf, vbuf, bias_ref):
    qb = q_ref.shape[1]
    da = q_ref.shape[2]
    hist = kbuf.shape[0] - qb
    t = pl.program_id(1)

    @pl.when((pl.program_id(0) == 0) & (t == 0))
    def _():
        gw = g_ref.shape[2]
        qrow = lax.broadcasted_iota(jnp.int32, (qb, hist + qb), 0)
        kcol = lax.broadcasted_iota(jnp.int32, (qb, hist + qb), 1)
        dchunk = kcol // CHUNK - qrow // CHUNK
        band = (dchunk >= 0) & (dchunk <= ATT_LEFT)
        for h in range(da // ATT_DH):
            gb = jnp.broadcast_to(g_ref[h], (qb, gw))
            skew = pltpu.roll(gb, gw - (qb - 1), 1, stride=1, stride_axis=0)
            bias_ref[h] = jnp.where(band, skew[:, :hist + qb] * LOG2E, NEG_BIG)

    @pl.when(t == 0)
    def _():
        kbuf[0:hist, :] = jnp.zeros((hist, da), BF16)
        vbuf[0:hist, :] = jnp.zeros((hist, da), BF16)

    @pl.when(t > 0)
    def _():
        kbuf[0:hist, :] = kbuf[qb:qb + hist, :]
        vbuf[0:hist, :] = vbuf[qb:qb + hist, :]

    kbuf[hist:hist + qb, :] = k_ref[0]
    vbuf[hist:hist + qb, :] = v_ref[0]

    nh = da // ATT_DH
    lane_head = lax.broadcasted_iota(jnp.int32, (qb, da), 1) // ATT_DH

    def attend(mask_start):
        q = q_ref[0]
        kk = kbuf[...]
        vv = vbuf[...]
        scores = [lax.dot_general(jnp.where(lane_head == h, q, jnp.zeros_like(q)), kk,
                                  (((1,), (1,)), ((), ())), preferred_element_type=F32)
                  for h in range(nh)]
        if mask_start:
            kidx = lax.broadcasted_iota(jnp.int32, (qb, hist + qb), 1)
            in_seq = kidx >= hist - t * qb
        out = jnp.zeros((qb, da), F32)
        for h in range(nh):
            s = scores[h] + bias_ref[h]
            if mask_start:
                s = jnp.where(in_seq, s, NEG_BIG)
            e = jnp.exp2(s - jnp.max(s, axis=-1, keepdims=True))
            l = jnp.sum(e, axis=-1, keepdims=True)
            oh = jnp.dot(e.astype(BF16), vv, preferred_element_type=F32)
            out = out + jnp.where(lane_head == h, oh / l, 0.0)
        o_ref[0] = out.astype(o_ref.dtype)

    @pl.when(t * qb < hist)
    def _():
        attend(True)

    @pl.when(t * qb >= hist)
    def _():
        attend(False)


def _att_rel_row(rel_table, qb):
    hist = ATT_LEFT * CHUNK
    gw = pl.next_power_of_2(2 * qb + hist - 1)
    n_lo = qb - 1 + hist - MAX_REL
    n_hi = gw - n_lo - (2 * MAX_REL + 1)
    nh = rel_table.shape[0]
    tab = rel_table.astype(F32)
    row = jnp.concatenate([jnp.broadcast_to(tab[:, :1], (nh, n_lo)), tab,
                           jnp.broadcast_to(tab[:, -1:], (nh, n_hi))], axis=1)
    return row.reshape(nh, 1, gw)


def _att_mixer(q, k, v, rel_table, qb):
    b, t, da = q.shape
    hist = ATT_LEFT * CHUNK
    grow = _att_rel_row(rel_table, qb)
    tile = pl.BlockSpec((1, qb, da), lambda i, j: (i, j, 0))
    return pl.pallas_call(
        _att_kernel,
        out_shape=jax.ShapeDtypeStruct((b, t, da), BF16),
        grid=(b, t // qb),
        in_specs=[tile, tile, tile, pl.BlockSpec(grow.shape, lambda i, j: (0, 0, 0))],
        out_specs=tile,
        scratch_shapes=[pltpu.VMEM((hist + qb, da), BF16),
                        pltpu.VMEM((hist + qb, da), BF16),
                        pltpu.VMEM((ATT_HEADS, qb, hist + qb), F32)],
        compiler_params=_cparams(("arbitrary", "arbitrary")),
        name="att_mixer",
    )(q, k, v, grow)


def _tail_kernel(x_ref, yc_ref, yr_ref, ya_ref, wo_ref, g1_ref, l1g_ref, l1b_ref,
                 sc_ref, sh_ref, g2_ref, w1_ref, w2_ref, l2g_ref, l2b_ref, o_ref, a_s, x1_s, h_s,
                 *, alpha, d_ff, fc):
    tm = x_ref.shape[1]
    groups = [slice(i * tm // TAIL_SPLIT, (i + 1) * tm // TAIL_SPLIT) for i in range(TAIL_SPLIT)]
    ys = [jnp.dot(jnp.concatenate([yc_ref[0, rs, :], yr_ref[0, rs, :], ya_ref[0, rs, :]], axis=-1),
                  wo_ref[...], preferred_element_type=F32) for rs in groups]
    for rs, y in zip(groups, ys):
        x1 = _layer_norm(alpha * x_ref[0, rs, :] + (1.0 + g1_ref[0]) * y, l1g_ref[...], l1b_ref[...])
        x1_s[rs, :] = x1
        h_s[rs, :] = (x1 * (1.0 + sc_ref[0]) + sh_ref[0]).astype(BF16)
    for c in range(d_ff // fc):
        for rs in groups:
            gt = jnp.dot(h_s[rs, :], w1_ref[:, c * fc:(c + 1) * fc], preferred_element_type=F32)
            up = jnp.dot(h_s[rs, :], w1_ref[:, d_ff + c * fc:d_ff + (c + 1) * fc],
                         preferred_element_type=F32)
            a_s[rs, c * fc:(c + 1) * fc] = (_silu(gt) * up).astype(BF16)
    y2s = [jnp.dot(a_s[rs, :], w2_ref[...], preferred_element_type=F32) for rs in groups]
    for rs, y2 in zip(groups, y2s):
        o_ref[0, rs, :] = _layer_norm(alpha * x1_s[rs, :] + (1.0 + g2_ref[0]) * y2,
                                      l2g_ref[...], l2b_ref[...])


def _tail(x, yc, yr, ya, wo_bf16, g1, ln1_g, ln1_b, sc2, sh2, g2, w1_bf16, w2_bf16, ln2_g, ln2_b,
          layer, alpha, tm):
    b, t, d = x.shape
    d_ff = w2_bf16.shape[1]
    tile = lambda a: pl.BlockSpec((1, tm, a.shape[2]), lambda i, j: (i, j, 0))
    mod = pl.BlockSpec((1, 1, d), lambda i, j: (i, 0, 0))
    row = pl.BlockSpec((1, d), lambda i, j: (0, 0))
    weight = lambda a: _layer_block(a, layer, pipeline_mode=pl.Buffered(1))
    return pl.pallas_call(
        functools.partial(_tail_kernel, alpha=alpha, d_ff=d_ff, fc=FFN_FC),
        out_shape=jax.ShapeDtypeStruct((b, t, d), F32),
        grid=(b, t // tm),
        in_specs=[tile(x), tile(yc), tile(yr), tile(ya), weight(wo_bf16), mod, row, row,
                  mod, mod, mod, weight(w1_bf16), weight(w2_bf16), row, row],
        out_specs=tile(x),
        scratch_shapes=[pltpu.VMEM((tm, d_ff), BF16),
                        pltpu.VMEM((tm, d), F32),
                        pltpu.VMEM((tm, d), BF16)],
        compiler_params=_cparams(("arbitrary", "arbitrary")),
        name="tail",
    )(x, yc, yr, ya, wo_bf16, g1, ln1_g.reshape(1, d), ln1_b.reshape(1, d), sc2, sh2, g2,
      w1_bf16, w2_bf16, ln2_g.reshape(1, d), ln2_b.reshape(1, d))


def kernel(x, c, w_ada, b_ada, w_in, conv_w, conv_b, conv_ln_g, conv_ln_b, rec_lower_bound,
           rec_norm_g, rel_bias, w_out, ln1_g, ln1_b, ln2_g, ln2_b, w_ffn_in, w_ffn_out):
    depth = w_in.shape[0]
    b, t, d = x.shape
    alpha = (2 * depth) ** 0.25

    lbs = jax.nn.softmax(rec_lower_bound.astype(F32), axis=0)
    lbs = jnp.cumsum(lbs, axis=0) - lbs[0]
    mods = _ada_mod(c, w_ada, b_ada)
    w_in, w_out, w_ffn_in, w_ffn_out = [w.astype(BF16) for w in (w_in, w_out, w_ffn_in, w_ffn_out)]

    for l in range(depth):
        sh1, sc1, g1, sh2, sc2, g2 = [m.reshape(b, 1, d) for m in jnp.split(mods[l], 6, axis=-1)]
        y_conv, rq, lf, rk, ri, rg, aq, ak, av = _front(
            x, sc1, sh1, w_in, l, conv_w[l], conv_b[l], conv_ln_g[l], conv_ln_b[l], lbs[l],
            tm=FRONT_ROWS)
        y_rec = _rec_mixer(rq, lf, rk, ri, rg, rec_norm_g[l], tb=REC_TB)
        y_att = _att_mixer(aq, ak, av, rel_bias[l], qb=ATT_QB)
        x = _tail(x, y_conv, y_rec, y_att, w_out, g1, ln1_g[l], ln1_b[l], sc2, sh2, g2,
                  w_ffn_in, w_ffn_out, ln2_g[l], ln2_b[l], l, alpha, tm=TILE_ROWS)
    return x
```

```python
import functools
import math

import numpy as np
import jax
import jax.numpy as jnp
from jax import lax
from jax.experimental import pallas as pl
from jax.experimental.pallas import tpu as pltpu

F32 = jnp.float32
BF16 = jnp.bfloat16

CHUNK = 64
CONV_WIDTH = 31
REC_HEADS = 4
REC_D = 128
ATT_HEADS = 4
ATT_DH = 64
ATT_LEFT = 8
MAX_REL = 128
LN_EPS = 1e-5
NEG_BIG = -1e30
TINY = 1e-30
LOG2E = math.log2(math.e)

FRONT_ROWS = 512
TILE_ROWS = 1024
REC_TB = 512
ATT_QB = 256
CONV_TAIL = 32
CONV_RS = 32
GATE_RS = 16
FFN_FC = 256
TAIL_SPLIT = 4
SUBLANES = 8

VMEM_LIMIT = 56 * 1024 * 1024


def _cparams(sem):
    return pltpu.CompilerParams(dimension_semantics=sem, vmem_limit_bytes=VMEM_LIMIT)


def _layer_norm(x, g, b):
    mu = jnp.mean(x, axis=-1, keepdims=True)
    xc = x - mu
    var = jnp.mean(xc * xc, axis=-1, keepdims=True)
    return xc * lax.rsqrt(var + LN_EPS) * g + b


def _silu(x):
    return x * jax.nn.sigmoid(x)


def _ada_kernel(c_ref, w_ref, b_ref, o_ref):
    ca = _silu(c_ref[...]).astype(BF16)
    o_ref[0] = jnp.dot(ca, w_ref[0].astype(BF16), preferred_element_type=F32) + b_ref[0]


def _ada_mod(c, w_ada, b_ada):
    depth, d, n6 = w_ada.shape
    b = c.shape[0]
    bp = ((b + 7) // 8) * 8
    cp = jnp.zeros((bp, d), F32).at[:b].set(c)
    tn = 1536
    out = pl.pallas_call(
        _ada_kernel,
        out_shape=jax.ShapeDtypeStruct((depth, bp, n6), F32),
        grid=(depth, n6 // tn),
        in_specs=[pl.BlockSpec((bp, d), lambda l, j: (0, 0)),
                  pl.BlockSpec((1, d, tn), lambda l, j: (l, 0, j)),
                  pl.BlockSpec((1, 1, tn), lambda l, j: (l, 0, j))],
        out_specs=pl.BlockSpec((1, bp, tn), lambda l, j: (l, 0, j)),
        compiler_params=_cparams(("arbitrary", "arbitrary")),
        name="ada_mod",
    )(cp, w_ada, b_ada.reshape(depth, 1, n6))
    return out[:, :b]


def _front_kernel(x_ref, sc_ref, sh_ref, w_ref, cw_ref, cb_ref, cg_ref, cbe_ref, lb_ref,
                  yc_ref, rq_ref, lf_ref, rk_ref, ri_ref, rg_ref, aq_ref, ak_ref, av_ref,
                  shs, h_s, z_s):
    tm = x_ref.shape[1]
    dc = cw_ref.shape[1]
    dr = lb_ref.shape[1]
    da = aq_ref.shape[2]
    rec0 = 2 * dc
    att0 = 2 * dc + 4 * dr
    quarter = tm // 4

    def stage(fn):
        fn()

    def proj(c0, n):
        return jnp.dot(h_s[...], w_ref[:, c0:c0 + n], preferred_element_type=F32)

    def paced_zero(dep):
        bits = lax.bitcast_convert_type(dep[0:SUBLANES, 0:128], jnp.uint32)
        zero = lax.shift_right_logical(lax.shift_right_logical(bits, jnp.uint32(16)), jnp.uint32(16))
        zero = lax.bitcast_convert_type(zero, F32)
        return jnp.tile(zero, (CONV_RS // SUBLANES, dc // 128))

    def conv_rows(lo, hi, dep):
        off = CONV_TAIL - (CONV_WIDTH - 1)
        acc = dep
        for r0 in range(lo, hi, CONV_RS):
            acc = paced_zero(acc)
            for j in range(CONV_WIDTH):
                a, s = divmod(off + j, SUBLANES)
                acc = acc + cw_ref[j:j + 1, :] * shs[s, r0 + a * SUBLANES:r0 + a * SUBLANES + CONV_RS, :]
            y = _layer_norm(acc + cb_ref[...], cg_ref[...], cbe_ref[...])
            yc_ref[0, r0:r0 + CONV_RS, :] = _silu(y).astype(BF16)

    def gate_rows(lo, hi):
        lb = lb_ref[...]
        lb_floor = jnp.maximum(lb, TINY)
        one_m_lb = 1.0 - lb
        for r0 in range(lo, hi, GATE_RS):
            z = z_s[r0:r0 + GATE_RS, :]
            e = jnp.exp(-jnp.abs(z))
            r = 1.0 / (1.0 + e)
            pos = z >= 0.0
            lf_ref[0, r0:r0 + GATE_RS, :] = jnp.log(lb_floor + one_m_lb * (jnp.where(pos, 1.0, e) * r))
            rk_ref[0, r0:r0 + GATE_RS, :] = one_m_lb * (jnp.where(pos, e, 1.0) * r)

    @pl.when(pl.program_id(1) == 0)
    def _():
        shs[0, 0:CONV_TAIL, :] = jnp.zeros((CONV_TAIL, dc), F32)

    @pl.when(pl.program_id(1) > 0)
    def _():
        shs[0, 0:CONV_TAIL, :] = shs[0, tm:tm + CONV_TAIL, :]

    h_s[...] = (x_ref[0] * (1.0 + sc_ref[0]) + sh_ref[0]).astype(BF16)

    @stage
    def _():
        pc = proj(0, 2 * dc)
        for r0 in range(0, tm, CONV_RS):
            blk = pc[r0:r0 + CONV_RS, :]
            shs[0, CONV_TAIL + r0:CONV_TAIL + r0 + CONV_RS, :] = (
                blk[:, :dc] * jax.nn.sigmoid(blk[:, dc:]))

    @stage
    def _():
        rq_ref[0] = proj(rec0, dr)
        n_sh = tm + CONV_TAIL - SUBLANES
        for s in range(1, SUBLANES):
            for r0 in range(0, n_sh, CONV_RS):
                n = min(CONV_RS, n_sh - r0)
                shs[s, r0:r0 + n, :] = shs[0, r0 + s:r0 + s + n, :]

    @stage
    def _():
        res = proj(rec0 + dr, dr)
        z_s[...] = res
        conv_rows(0, quarter, res)

    @stage
    def _():
        res = proj(rec0 + 2 * dr, dr)
        ri_ref[0] = res.astype(BF16)
        conv_rows(quarter, 2 * quarter, res)
        gate_rows(0, 2 * quarter)

    @stage
    def _():
        res = proj(rec0 + 3 * dr, dr)
        rg_ref[0] = res
        conv_rows(2 * quarter, 3 * quarter, res)
        gate_rows(2 * quarter, tm)

    @stage
    def _():
        aq_ref[0] = (proj(att0, da) * (LOG2E * ATT_DH ** -0.5)).astype(BF16)
        ak_ref[0] = proj(att0 + da, da).astype(BF16)
        res = proj(att0 + 2 * da, da)
        av_ref[0] = res.astype(BF16)
        conv_rows(3 * quarter, tm, res)


def _layer_block(stacked, layer, **kw):
    return pl.BlockSpec((None,) + stacked.shape[1:], lambda i, j: (layer, 0, 0), **kw)


def _front(x, sc, sh, w_all_bf16, layer, conv_w, conv_b, conv_ln_g, conv_ln_b, lb, tm):
    b, t, d = x.shape
    dc = conv_w.shape[1]
    dr = lb.shape[0]
    da = ATT_HEADS * ATT_DH
    row = lambda a: a.reshape(1, -1)
    const = lambda a: pl.BlockSpec(a.shape, lambda i, j: (0,) * a.ndim)
    mod = pl.BlockSpec((1, 1, d), lambda i, j: (i, 0, 0))
    tile = lambda n: pl.BlockSpec((1, tm, n), lambda i, j: (i, j, 0))
    outs = [(dc, BF16), (dr, F32), (dr, F32), (dr, F32), (dr, BF16), (dr, F32),
            (da, BF16), (da, BF16), (da, BF16)]
    consts = [conv_w, row(conv_b), row(conv_ln_g), row(conv_ln_b), row(lb)]
    return pl.pallas_call(
        _front_kernel,
        out_shape=[jax.ShapeDtypeStruct((b, t, n), dt) for n, dt in outs],
        grid=(b, t // tm),
        in_specs=[tile(d), mod, mod, _layer_block(w_all_bf16, layer)] + [const(a) for a in consts],
        out_specs=[tile(n) for n, _ in outs],
        scratch_shapes=[pltpu.VMEM((SUBLANES, tm + CONV_TAIL, dc), F32),
                        pltpu.VMEM((tm, d), BF16),
                        pltpu.VMEM((tm, dr), F32)],
        compiler_params=_cparams(("arbitrary", "arbitrary")),
        name="front",
    )(x, sc, sh, w_all_bf16, *consts)


REC_LEVELS = tuple(CHUNK >> (i + 1) for i in range(int(math.log2(CHUNK))))
REC_SAFE_RANGE = 150.0
REC_SPLIT = 2


def _rec_tables():
    x = np.arange(CHUNK)
    sel = [x[:, None] >= x[None, :]]
    masks = []
    for m in REC_LEVELS:
        bnd = (x // (2 * m)) * (2 * m) + m - 1
        sel.append(bnd[:, None] >= x[None, :])
        same = (x[:, None] // (2 * m)) == (x[None, :] // (2 * m))
        masks.append(same & ((x[:, None] & m) != 0) & ((x[None, :] & m) == 0))
    masks.append(x[:, None] == x[None, :])
    sel = np.concatenate(sel, axis=0).astype(np.float32)
    return np.concatenate([sel] * REC_SPLIT, axis=1), np.stack(masks).astype(np.float32)


def _split_bf16(x):
    parts = []
    for _ in range(REC_SPLIT):
        p = x.astype(BF16)
        parts.append(p)
        x = x - p.astype(F32)
    return parts


def _rec_kernel(q_ref, lf_ref, k_ref, i_ref, g_ref, ng_ref, sel_ref, pm_ref, o_ref, st_ref):
    tb = q_ref.shape[1]
    dd = q_ref.shape[2]
    nh = dd // REC_D
    nchunks = tb // CHUNK
    nt = (((1,), (1,)), ((), ()))
    tn = (((0,), (0,)), ((), ()))

    @pl.when(pl.program_id(1) == 0)
    def _():
        st_ref[...] = jnp.zeros(st_ref.shape, F32)

    tot = [jnp.sum(lf_ref[0, c * CHUNK:(c + 1) * CHUNK, :], axis=0, keepdims=True)
           for c in range(nchunks)]
    safe = jnp.min(functools.reduce(jnp.minimum, tot)) >= -REC_SAFE_RANGE

    def prefix(lf, nrows):
        return jnp.dot(sel_ref[0:nrows, :], jnp.concatenate(_split_bf16(lf), axis=0),
                       preferred_element_type=F32)

    def head_out(h, a, qg, kg, vb, eb):
        sl = slice(h * REC_D, (h + 1) * REC_D)
        st = st_ref[h]
        o_h = (lax.dot_general(qg[:, sl], st.astype(BF16), nt, preferred_element_type=F32)
               + jnp.dot(a.astype(BF16), vb[:, sl], preferred_element_type=F32))
        upd = lax.dot_general(vb[:, sl], kg[:, sl], tn, preferred_element_type=F32)
        st_ref[h] = st * eb[:, sl] + upd
        ms = jnp.mean(o_h * o_h, axis=-1, keepdims=True)
        return o_h * lax.rsqrt(ms + LN_EPS) * ng_ref[...]

    def store(rs, outs):
        y = jnp.concatenate(outs, axis=-1) * _silu(g_ref[0, rs, :])
        o_ref[0, rs, :] = y.astype(o_ref.dtype)

    @pl.when(safe)
    def _():
        causal = (lax.broadcasted_iota(jnp.int32, (CHUNK, CHUNK), 0)
                  >= lax.broadcasted_iota(jnp.int32, (CHUNK, CHUNK), 1))
        heads = [slice(h * REC_D, (h + 1) * REC_D) for h in range(nh)]
        rows = [slice(c * CHUNK, (c + 1) * CHUNK) for c in range(nchunks)]
        bcs = [prefix(lf_ref[0, rs, :], CHUNK) for rs in rows]
        qg, kg, vb, eb, att, upd = [], [], [], [], [], []
        for rs, bc in zip(rows, bcs):
            half = 0.5 * bc[CHUNK - 1:CHUNK, :]
            eh = jnp.exp(half)
            qf = q_ref[0, rs, :] * jnp.exp(bc - half)
            kf = k_ref[0, rs, :] * jnp.exp(half - bc)
            qi = qf.astype(BF16)
            ki = kf.astype(BF16)
            qg.append((qf * eh).astype(BF16))
            kg.append((kf * eh).astype(BF16))
            vb.append(i_ref[0, rs, :])
            eb.append(eh * eh)
            att.append([jnp.where(causal, lax.dot_general(qi[:, sl], ki[:, sl], nt,
                                                          preferred_element_type=F32),
                                  0.0).astype(BF16) for sl in heads])
        for c in range(nchunks):
            upd.append([lax.dot_general(vb[c][:, sl], kg[c][:, sl], tn, preferred_element_type=F32)
                        for sl in heads])
        states = []
        for h, sl in enumerate(heads):
            st = st_ref[h]
            per_chunk = []
            for c in range(nchunks):
                per_chunk.append(st.astype(BF16))
                st = st * eb[c][:, sl] + upd[c][h]
            st_ref[h] = st
            states.append(per_chunk)
        for c, rs in enumerate(rows):
            outs = []
            for h, sl in enumerate(heads):
                o_h = (lax.dot_general(qg[c][:, sl], states[h][c], nt, preferred_element_type=F32)
                       + jnp.dot(att[c][h], vb[c][:, sl], preferred_element_type=F32))
                ms = jnp.mean(o_h * o_h, axis=-1, keepdims=True)
                outs.append(o_h * lax.rsqrt(ms + LN_EPS) * ng_ref[...])
            store(rs, outs)

    @pl.when(jnp.logical_not(safe))
    def _():
        row_id = lax.broadcasted_iota(jnp.int32, (CHUNK, dd), 0)

        def chunk_body(ci, carry):
            rs = pl.ds(pl.multiple_of(ci * CHUNK, CHUNK), CHUNK)
            q = q_ref[0, rs, :]
            k = k_ref[0, rs, :]
            pre = prefix(lf_ref[0, rs, :], sel_ref.shape[0])
            bc = pre[0:CHUNK, :]
            bl = bc[CHUNK - 1:CHUNK, :]
            qg = (q * jnp.exp(bc)).astype(BF16)
            kg = (k * jnp.exp(bl - bc)).astype(BF16)
            vb = i_ref[0, rs, :]
            xs = []
            for li, m in enumerate(REC_LEVELS):
                dec = jnp.exp(-jnp.abs(bc - pre[(li + 1) * CHUNK:(li + 2) * CHUNK, :]))
                xs.append((jnp.where((row_id & m) != 0, q, k) * dec).astype(BF16))
            qb16 = q.astype(BF16)
            kb16 = k.astype(BF16)
            outs = []
            for h in range(nh):
                sl = slice(h * REC_D, (h + 1) * REC_D)
                a = (lax.dot_general(qb16[:, sl], kb16[:, sl], nt, preferred_element_type=F32)
                     * pm_ref[len(REC_LEVELS)])
                for li in range(len(REC_LEVELS)):
                    xh = xs[li][:, sl]
                    a = a + lax.dot_general(xh, xh, nt, preferred_element_type=F32) * pm_ref[li]
                outs.append(head_out(h, a, qg, kg, vb, jnp.exp(bl)))
            store(rs, outs)
            return carry

        lax.fori_loop(0, nchunks, chunk_body, 0)


def _rec_mixer(q, lf, k, v, g, norm_g, tb):
    b, t, dd = q.shape
    nh = dd // REC_D
    sel, pmask = _rec_tables()
    sel = jnp.asarray(sel, BF16)
    pmask = jnp.asarray(pmask, F32)
    tile = pl.BlockSpec((1, tb, dd), lambda i, j: (i, j, 0))
    return pl.pallas_call(
        _rec_kernel,
        out_shape=jax.ShapeDtypeStruct((b, t, dd), BF16),
        grid=(b, t // tb),
        in_specs=[tile, tile, tile, tile, tile,
                  pl.BlockSpec((1, REC_D), lambda i, j: (0, 0)),
                  pl.BlockSpec(sel.shape, lambda i, j: (0, 0)),
                  pl.BlockSpec(pmask.shape, lambda i, j: (0, 0, 0))],
        out_specs=tile,
        scratch_shapes=[pltpu.VMEM((nh, REC_D, REC_D), F32)],
        compiler_params=_cparams(("arbitrary", "arbitrary")),
        name="rec_mixer",
    )(q, lf, k, v, g, norm_g.reshape(1, REC_D), sel, pmask)


def _att_kernel(q_ref, k_ref, v_ref, g_ref, o_ref, kbuf, vbuf, bias_ref):
    qb = q_ref.shape[1]
    da = q_ref.shape[2]
    hist = kbuf.shape[0] - qb
    t = pl.program_id(1)

    @pl.when((pl.program_id(0) == 0) & (t == 0))
    def _():
        gw = g_ref.shape[2]
        qrow = lax.broadcasted_iota(jnp.int32, (qb, hist + qb), 0)
        kcol = lax.broadcasted_iota(jnp.int32, (qb, hist + qb), 1)
        dchunk = kcol // CHUNK - qrow // CHUNK
        band = (dchunk >= 0) & (dchunk <= ATT_LEFT)
        for h in range(da // ATT_DH):
            gb = jnp.broadcast_to(g_ref[h], (qb, gw))
            skew = pltpu.roll(gb, gw - (qb - 1), 1, stride=1, stride_axis=0)
            bias_ref[h] = jnp.where(band, skew[:, :hist + qb] * LOG2E, NEG_BIG)

    @pl.when(t == 0)
    def _():
        kbuf[0:hist, :] = jnp.zeros((hist, da), BF16)
        vbuf[0:hist, :] = jnp.zeros((hist, da), BF16)

    @pl.when(t > 0)
    def _():
        kbuf[0:hist, :] = kbuf[qb:qb + hist, :]
        vbuf[0:hist, :] = vbuf[qb:qb + hist, :]

    kbuf[hist:hist + qb, :] = k_ref[0]
    vbuf[hist:hist + qb, :] = v_ref[0]

    nh = da // ATT_DH
    lane_head = lax.broadcasted_iota(jnp.int32, (qb, da), 1) // ATT_DH

    def attend(mask_start):
        q = q_ref[0]
        kk = kbuf[...]
        vv = vbuf[...]
        scores = [lax.dot_general(jnp.where(lane_head == h, q, jnp.zeros_like(q)), kk,
                                  (((1,), (1,)), ((), ())), preferred_element_type=F32)
                  for h in range(nh)]
        if mask_start:
            kidx = lax.broadcasted_iota(jnp.int32, (qb, hist + qb), 1)
            in_seq = kidx >= hist - t * qb
        out = jnp.zeros((qb, da), F32)
        for h in range(nh):
            s = scores[h] + bias_ref[h]
            if mask_start:
                s = jnp.where(in_seq, s, NEG_BIG)
            e = jnp.exp2(s - jnp.max(s, axis=-1, keepdims=True))
            l = jnp.sum(e, axis=-1, keepdims=True)
            oh = jnp.dot(e.astype(BF16), vv, preferred_element_type=F32)
            out = out + jnp.where(lane_head == h, oh / l, 0.0)
        o_ref[0] = out.astype(o_ref.dtype)

    @pl.when(t * qb < hist)
    def _():
        attend(True)

    @pl.when(t * qb >= hist)
    def _():
        attend(False)


def _att_rel_row(rel_table, qb):
    hist = ATT_LEFT * CHUNK
    gw = pl.next_power_of_2(2 * qb + hist - 1)
    n_lo = qb - 1 + hist - MAX_REL
    n_hi = gw - n_lo - (2 * MAX_REL + 1)
    nh = rel_table.shape[0]
    tab = rel_table.astype(F32)
    row = jnp.concatenate([jnp.broadcast_to(tab[:, :1], (nh, n_lo)), tab,
                           jnp.broadcast_to(tab[:, -1:], (nh, n_hi))], axis=1)
    return row.reshape(nh, 1, gw)


def _att_mixer(q, k, v, rel_table, qb):
    b, t, da = q.shape
    hist = ATT_LEFT * CHUNK
    grow = _att_rel_row(rel_table, qb)
    tile = pl.BlockSpec((1, qb, da), lambda i, j: (i, j, 0))
    return pl.pallas_call(
        _att_kernel,
        out_shape=jax.ShapeDtypeStruct((b, t, da), BF16),
        grid=(b, t // qb),
        in_specs=[tile, tile, tile, pl.BlockSpec(grow.shape, lambda i, j: (0, 0, 0))],
        out_specs=tile,
        scratch_shapes=[pltpu.VMEM((hist + qb, da), BF16),
                        pltpu.VMEM((hist + qb, da), BF16),
                        pltpu.VMEM((ATT_HEADS, qb, hist + qb), F32)],
        compiler_params=_cparams(("arbitrary", "arbitrary")),
        name="att_mixer",
    )(q, k, v, grow)


def _tail_kernel(x_ref, yc_ref, yr_ref, ya_ref, wo_ref, g1_ref, l1g_ref, l1b_ref,
                 sc_ref, sh_ref, g2_ref, w1_ref, w2_ref, l2g_ref, l2b_ref, o_ref, a_s, x1_s, h_s,
                 *, alpha, d_ff, fc):
    tm = x_ref.shape[1]
    groups = [slice(i * tm // TAIL_SPLIT, (i + 1) * tm // TAIL_SPLIT) for i in range(TAIL_SPLIT)]
    ys = [jnp.dot(jnp.concatenate([yc_ref[0, rs, :], yr_ref[0, rs, :], ya_ref[0, rs, :]], axis=-1),
                  wo_ref[...], preferred_element_type=F32) for rs in groups]
    for rs, y in zip(groups, ys):
        x1 = _layer_norm(alpha * x_ref[0, rs, :] + (1.0 + g1_ref[0]) * y, l1g_ref[...], l1b_ref[...])
        x1_s[rs, :] = x1
        h_s[rs, :] = (x1 * (1.0 + sc_ref[0]) + sh_ref[0]).astype(BF16)
    for c in range(d_ff // fc):
        for rs in groups:
            gt = jnp.dot(h_s[rs, :], w1_ref[:, c * fc:(c + 1) * fc], preferred_element_type=F32)
            up = jnp.dot(h_s[rs, :], w1_ref[:, d_ff + c * fc:d_ff + (c + 1) * fc],
                         preferred_element_type=F32)
            a_s[rs, c * fc:(c + 1) * fc] = (_silu(gt) * up).astype(BF16)
    y2s = [jnp.dot(a_s[rs, :], w2_ref[...], preferred_element_type=F32) for rs in groups]
    for rs, y2 in zip(groups, y2s):
        o_ref[0, rs, :] = _layer_norm(alpha * x1_s[rs, :] + (1.0 + g2_ref[0]) * y2,
                                      l2g_ref[...], l2b_ref[...])


def _tail(x, yc, yr, ya, wo_bf16, g1, ln1_g, ln1_b, sc2, sh2, g2, w1_bf16, w2_bf16, ln2_g, ln2_b,
          layer, alpha, tm):
    b, t, d = x.shape
    d_ff = w2_bf16.shape[1]
    tile = lambda a: pl.BlockSpec((1, tm, a.shape[2]), lambda i, j: (i, j, 0))
    mod = pl.BlockSpec((1, 1, d), lambda i, j: (i, 0, 0))
    row = pl.BlockSpec((1, d), lambda i, j: (0, 0))
    weight = lambda a: _layer_block(a, layer, pipeline_mode=pl.Buffered(1))
    return pl.pallas_call(
        functools.partial(_tail_kernel, alpha=alpha, d_ff=d_ff, fc=FFN_FC),
        out_shape=jax.ShapeDtypeStruct((b, t, d), F32),
        grid=(b, t // tm),
        in_specs=[tile(x), tile(yc), tile(yr), tile(ya), weight(wo_bf16), mod, row, row,
                  mod, mod, mod, weight(w1_bf16), weight(w2_bf16), row, row],
        out_specs=tile(x),
        scratch_shapes=[pltpu.VMEM((tm, d_ff), BF16),
                        pltpu.VMEM((tm, d), F32),
                        pltpu.VMEM((tm, d), BF16)],
        compiler_params=_cparams(("arbitrary", "arbitrary")),
        name="tail",
    )(x, yc, yr, ya, wo_bf16, g1, ln1_g.reshape(1, d), ln1_b.reshape(1, d), sc2, sh2, g2,
      w1_bf16, w2_bf16, ln2_g.reshape(1, d), ln2_b.reshape(1, d))


def kernel(x, c, w_ada, b_ada, w_in, conv_w, conv_b, conv_ln_g, conv_ln_b, rec_lower_bound,
           rec_norm_g, rel_bias, w_out, ln1_g, ln1_b, ln2_g, ln2_b, w_ffn_in, w_ffn_out):
    depth = w_in.shape[0]
    b, t, d = x.shape
    alpha = (2 * depth) ** 0.25

    lbs = jax.nn.softmax(rec_lower_bound.astype(F32), axis=0)
    lbs = jnp.cumsum(lbs, axis=0) - lbs[0]
    mods = _ada_mod(c, w_ada, b_ada)
    w_in, w_out, w_ffn_in, w_ffn_out = [w.astype(BF16) for w in (w_in, w_out, w_ffn_in, w_ffn_out)]

    for l in range(depth):
        sh1, sc1, g1, sh2, sc2, g2 = [m.reshape(b, 1, d) for m in jnp.split(mods[l], 6, axis=-1)]
        y_conv, rq, lf, rk, ri, rg, aq, ak, av = _front(
            x, sc1, sh1, w_in, l, conv_w[l], conv_b[l], conv_ln_g[l], conv_ln_b[l], lbs[l],
            tm=FRONT_ROWS)
        y_rec = _rec_mixer(rq, lf, rk, ri, rg, rec_norm_g[l], tb=REC_TB)
        y_att = _att_mixer(aq, ak, av, rel_bias[l], qb=ATT_QB)
        x = _tail(x, y_conv, y_rec, y_att, w_out, g1, ln1_g[l], ln1_b[l], sc2, sh2, g2,
                  w_ffn_in, w_ffn_out, ln2_g[l], ln2_b[l], l, alpha, tm=TILE_ROWS)
    return x
```

```python
import functools
import math

import numpy as np
import jax
import jax.numpy as jnp
from jax import lax
from jax.experimental import pallas as pl
from jax.experimental.pallas import tpu as pltpu

F32 = jnp.float32
BF16 = jnp.bfloat16

CHUNK = 64
CONV_WIDTH = 31
REC_HEADS = 4
REC_D = 128
ATT_HEADS = 4
ATT_DH = 64
ATT_LEFT = 8
MAX_REL = 128
LN_EPS = 1e-5
NEG_BIG = -1e30
TINY = 1e-30
LOG2E = math.log2(math.e)

FRONT_ROWS = 512
TILE_ROWS = 1024
REC_TB = 512
ATT_QB = 256
CONV_TAIL = 32
CONV_RS = 32
GATE_RS = 16
FFN_FC = 256
TAIL_SPLIT = 4
SUBLANES = 8

VMEM_LIMIT = 56 * 1024 * 1024


def _cparams(sem):
    return pltpu.CompilerParams(dimension_semantics=sem, vmem_limit_bytes=VMEM_LIMIT)


def _layer_norm(x, g, b):
    mu = jnp.mean(x, axis=-1, keepdims=True)
    xc = x - mu
    var = jnp.mean(xc * xc, axis=-1, keepdims=True)
    return xc * lax.rsqrt(var + LN_EPS) * g + b


def _silu(x):
    return x * jax.nn.sigmoid(x)


def _ada_kernel(c_ref, w_ref, b_ref, o_ref):
    ca = _silu(c_ref[...]).astype(BF16)
    o_ref[0] = jnp.dot(ca, w_ref[0].astype(BF16), preferred_element_type=F32) + b_ref[0]


def _ada_mod(c, w_ada, b_ada):
    depth, d, n6 = w_ada.shape
    b = c.shape[0]
    bp = ((b + 7) // 8) * 8
    cp = jnp.zeros((bp, d), F32).at[:b].set(c)
    tn = 1536
    out = pl.pallas_call(
        _ada_kernel,
        out_shape=jax.ShapeDtypeStruct((depth, bp, n6), F32),
        grid=(depth, n6 // tn),
        in_specs=[pl.BlockSpec((bp, d), lambda l, j: (0, 0)),
                  pl.BlockSpec((1, d, tn), lambda l, j: (l, 0, j)),
                  pl.BlockSpec((1, 1, tn), lambda l, j: (l, 0, j))],
        out_specs=pl.BlockSpec((1, bp, tn), lambda l, j: (l, 0, j)),
        compiler_params=_cparams(("arbitrary", "arbitrary")),
        name="ada_mod",
    )(cp, w_ada, b_ada.reshape(depth, 1, n6))
    return out[:, :b]


def _front_kernel(x_ref, sc_ref, sh_ref, w_ref, cw_ref, cb_ref, cg_ref, cbe_ref, lb_ref,
                  yc_ref, rf_ref, ri_ref, aq_ref, ak_ref, av_ref, shs, h_s, z_s):
    rq_ref, lf_ref, rk_ref, rg_ref = _rec_streams(rf_ref)
    tm = x_ref.shape[1]
    dc = cw_ref.shape[1]
    dr = lb_ref.shape[1]
    da = aq_ref.shape[2]
    rec0 = 2 * dc
    att0 = 2 * dc + 4 * dr
    quarter = tm // 4

    def stage(fn):
        fn()

    def proj(c0, n):
        return jnp.dot(h_s[...], w_ref[:, c0:c0 + n], preferred_element_type=F32)

    def paced_zero(dep):
        bits = lax.bitcast_convert_type(dep[0:SUBLANES, 0:128], jnp.uint32)
        zero = lax.shift_right_logical(lax.shift_right_logical(bits, jnp.uint32(16)), jnp.uint32(16))
        zero = lax.bitcast_convert_type(zero, F32)
        return jnp.tile(zero, (CONV_RS // SUBLANES, dc // 128))

    def conv_rows(lo, hi, dep):
        off = CONV_TAIL - (CONV_WIDTH - 1)
        acc = dep
        for r0 in range(lo, hi, CONV_RS):
            acc = paced_zero(acc)
            for j in range(CONV_WIDTH):
                a, s = divmod(off + j, SUBLANES)
                acc = acc + cw_ref[j:j + 1, :] * shs[s, r0 + a * SUBLANES:r0 + a * SUBLANES + CONV_RS, :]
            y = _layer_norm(acc + cb_ref[...], cg_ref[...], cbe_ref[...])
            yc_ref[0, r0:r0 + CONV_RS, :] = _silu(y).astype(BF16)

    def gate_rows(lo, hi):
        lb = lb_ref[...]
        lb_floor = jnp.maximum(lb, TINY)
        one_m_lb = 1.0 - lb
        for r0 in range(lo, hi, GATE_RS):
            z = z_s[r0:r0 + GATE_RS, :]
            e = jnp.exp(-jnp.abs(z))
            r = 1.0 / (1.0 + e)
            pos = z >= 0.0
            lf_ref[0, r0:r0 + GATE_RS, :] = jnp.log(lb_floor + one_m_lb * (jnp.where(pos, 1.0, e) * r))
            rk_ref[0, r0:r0 + GATE_RS, :] = one_m_lb * (jnp.where(pos, e, 1.0) * r)

    @pl.when(pl.program_id(1) == 0)
    def _():
        shs[0, 0:CONV_TAIL, :] = jnp.zeros((CONV_TAIL, dc), F32)

    @pl.when(pl.program_id(1) > 0)
    def _():
        shs[0, 0:CONV_TAIL, :] = shs[0, tm:tm + CONV_TAIL, :]

    h_s[...] = (x_ref[0] * (1.0 + sc_ref[0]) + sh_ref[0]).astype(BF16)

    @stage
    def _():
        pc = proj(0, 2 * dc)
        for r0 in range(0, tm, CONV_RS):
            blk = pc[r0:r0 + CONV_RS, :]
            shs[0, CONV_TAIL + r0:CONV_TAIL + r0 + CONV_RS, :] = (
                blk[:, :dc] * jax.nn.sigmoid(blk[:, dc:]))

    @stage
    def _():
        rq_ref[0] = proj(rec0, dr)
        n_sh = tm + CONV_TAIL - SUBLANES
        for s in range(1, SUBLANES):
            for r0 in range(0, n_sh, CONV_RS):
                n = min(CONV_RS, n_sh - r0)
                shs[s, r0:r0 + n, :] = shs[0, r0 + s:r0 + s + n, :]

    @stage
    def _():
        res = proj(rec0 + dr, dr)
        z_s[...] = res
        conv_rows(0, quarter, res)

    @stage
    def _():
        res = proj(rec0 + 2 * dr, dr)
        ri_ref[0] = res.astype(BF16)
        conv_rows(quarter, 2 * quarter, res)
        gate_rows(0, 2 * quarter)

    @stage
    def _():
        res = proj(rec0 + 3 * dr, dr)
        rg_ref[0] = res
        conv_rows(2 * quarter, 3 * quarter, res)
        gate_rows(2 * quarter, tm)

    @stage
    def _():
        aq_ref[0] = (proj(att0, da) * (LOG2E * ATT_DH ** -0.5)).astype(BF16)
        ak_ref[0] = proj(att0 + da, da).astype(BF16)
        res = proj(att0 + 2 * da, da)
        av_ref[0] = res.astype(BF16)
        conv_rows(3 * quarter, tm, res)


def _layer_block(stacked, layer, **kw):
    return pl.BlockSpec((None,) + stacked.shape[1:], lambda i, j: (layer, 0, 0), **kw)


def _front(x, sc, sh, w_all_bf16, layer, conv_w, conv_b, conv_ln_g, conv_ln_b, lb, tm):
    b, t, d = x.shape
    dc = conv_w.shape[1]
    dr = lb.shape[0]
    da = ATT_HEADS * ATT_DH
    row = lambda a: a.reshape(1, -1)
    const = lambda a: pl.BlockSpec(a.shape, lambda i, j: (0,) * a.ndim)
    mod = pl.BlockSpec((1, 1, d), lambda i, j: (i, 0, 0))
    tile = lambda n: pl.BlockSpec((1, tm, n), lambda i, j: (i, j, 0))
    outs = [(dc, BF16), (4 * dr, F32), (dr, BF16), (da, BF16), (da, BF16), (da, BF16)]
    consts = [conv_w, row(conv_b), row(conv_ln_g), row(conv_ln_b), row(lb)]
    return pl.pallas_call(
        _front_kernel,
        out_shape=[jax.ShapeDtypeStruct((b, t, n), dt) for n, dt in outs],
        grid=(b, t // tm),
        in_specs=[tile(d), mod, mod, _layer_block(w_all_bf16, layer)] + [const(a) for a in consts],
        out_specs=[tile(n) for n, _ in outs],
        scratch_shapes=[pltpu.VMEM((SUBLANES, tm + CONV_TAIL, dc), F32),
                        pltpu.VMEM((tm, d), BF16),
                        pltpu.VMEM((tm, dr), F32)],
        compiler_params=_cparams(("arbitrary", "arbitrary")),
        name="front",
    )(x, sc, sh, w_all_bf16, *consts)


REC_LEVELS = tuple(CHUNK >> (i + 1) for i in range(int(math.log2(CHUNK))))
REC_SAFE_RANGE = 150.0
REC_SPLIT = 2


def _rec_tables():
    x = np.arange(CHUNK)
    sel = [x[:, None] >= x[None, :]]
    masks = []
    for m in REC_LEVELS:
        bnd = (x // (2 * m)) * (2 * m) + m - 1
        sel.append(bnd[:, None] >= x[None, :])
        same = (x[:, None] // (2 * m)) == (x[None, :] // (2 * m))
        masks.append(same & ((x[:, None] & m) != 0) & ((x[None, :] & m) == 0))
    masks.append(x[:, None] == x[None, :])
    sel = np.concatenate(sel, axis=0).astype(np.float32)
    return np.concatenate([sel] * REC_SPLIT, axis=1), np.stack(masks).astype(np.float32)


def _split_bf16(x):
    parts = []
    for _ in range(REC_SPLIT):
        p = x.astype(BF16)
        parts.append(p)
        x = x - p.astype(F32)
    return parts


def _rec_streams(rf_ref):
    dd = rf_ref.shape[2] // 4
    return [rf_ref.at[:, :, i * dd:(i + 1) * dd] for i in range(4)]


def _rec_kernel(rf_ref, i_ref, ng_ref, sel_ref, pm_ref, o_ref, st_ref):
    q_ref, lf_ref, k_ref, g_ref = _rec_streams(rf_ref)
    tb = q_ref.shape[1]
    dd = q_ref.shape[2]
    nh = dd // REC_D
    nchunks = tb // CHUNK
    nt = (((1,), (1,)), ((), ()))
    tn = (((0,), (0,)), ((), ()))

    @pl.when(pl.program_id(1) == 0)
    def _():
        st_ref[...] = jnp.zeros(st_ref.shape, F32)

    tot = [jnp.sum(lf_ref[0, c * CHUNK:(c + 1) * CHUNK, :], axis=0, keepdims=True)
           for c in range(nchunks)]
    safe = jnp.min(functools.reduce(jnp.minimum, tot)) >= -REC_SAFE_RANGE

    def prefix(lf, nrows):
        return jnp.dot(sel_ref[0:nrows, :], jnp.concatenate(_split_bf16(lf), axis=0),
                       preferred_element_type=F32)

    def head_out(h, a, qg, kg, vb, eb):
        sl = slice(h * REC_D, (h + 1) * REC_D)
        st = st_ref[h]
        o_h = (lax.dot_general(qg[:, sl], st.astype(BF16), nt, preferred_element_type=F32)
               + jnp.dot(a.astype(BF16), vb[:, sl], preferred_element_type=F32))
        upd = lax.dot_general(vb[:, sl], kg[:, sl], tn, preferred_element_type=F32)
        st_ref[h] = st * eb[:, sl] + upd
        ms = jnp.mean(o_h * o_h, axis=-1, keepdims=True)
        return o_h * lax.rsqrt(ms + LN_EPS) * ng_ref[...]

    def store(rs, outs):
        y = jnp.concatenate(outs, axis=-1) * _silu(g_ref[0, rs, :])
        o_ref[0, rs, :] = y.astype(o_ref.dtype)

    @pl.when(safe)
    def _():
        causal = (lax.broadcasted_iota(jnp.int32, (CHUNK, CHUNK), 0)
                  >= lax.broadcasted_iota(jnp.int32, (CHUNK, CHUNK), 1))
        heads = [slice(h * REC_D, (h + 1) * REC_D) for h in range(nh)]
        rows = [slice(c * CHUNK, (c + 1) * CHUNK) for c in range(nchunks)]
        bcs = [prefix(lf_ref[0, rs, :], CHUNK) for rs in rows]
        qg, kg, vb, eb, att, upd = [], [], [], [], [], []
        for rs, bc in zip(rows, bcs):
            half = 0.5 * bc[CHUNK - 1:CHUNK, :]
            eh = jnp.exp(half)
            qf = q_ref[0, rs, :] * jnp.exp(bc - half)
            kf = k_ref[0, rs, :] * jnp.exp(half - bc)
            qi = qf.astype(BF16)
            ki = kf.astype(BF16)
            qg.append((qf * eh).astype(BF16))
            kg.append((kf * eh).astype(BF16))
            vb.append(i_ref[0, rs, :])
            eb.append(eh * eh)
            att.append([jnp.where(causal, lax.dot_general(qi[:, sl], ki[:, sl], nt,
                                                          preferred_element_type=F32),
                                  0.0).astype(BF16) for sl in heads])
        for c in range(nchunks):
            upd.append([lax.dot_general(vb[c][:, sl], kg[c][:, sl], tn, preferred_element_type=F32)
                        for sl in heads])
        states = []
        for h, sl in enumerate(heads):
            st = st_ref[h]
            per_chunk = []
            for c in range(nchunks):
                per_chunk.append(st.astype(BF16))
                st = st * eb[c][:, sl] + upd[c][h]
            st_ref[h] = st
            states.append(per_chunk)
        for c, rs in enumerate(rows):
            outs = []
            for h, sl in enumerate(heads):
                o_h = (lax.dot_general(qg[c][:, sl], states[h][c], nt, preferred_element_type=F32)
                       + jnp.dot(att[c][h], vb[c][:, sl], preferred_element_type=F32))
                ms = jnp.mean(o_h * o_h, axis=-1, keepdims=True)
                outs.append(o_h * lax.rsqrt(ms + LN_EPS) * ng_ref[...])
            store(rs, outs)

    @pl.when(jnp.logical_not(safe))
    def _():
        row_id = lax.broadcasted_iota(jnp.int32, (CHUNK, dd), 0)

        def chunk_body(ci, carry):
            rs = pl.ds(pl.multiple_of(ci * CHUNK, CHUNK), CHUNK)
            q = q_ref[0, rs, :]
            k = k_ref[0, rs, :]
            pre = prefix(lf_ref[0, rs, :], sel_ref.shape[0])
            bc = pre[0:CHUNK, :]
            bl = bc[CHUNK - 1:CHUNK, :]
            qg = (q * jnp.exp(bc)).astype(BF16)
            kg = (k * jnp.exp(bl - bc)).astype(BF16)
            vb = i_ref[0, rs, :]
            xs = []
            for li, m in enumerate(REC_LEVELS):
                dec = jnp.exp(-jnp.abs(bc - pre[(li + 1) * CHUNK:(li + 2) * CHUNK, :]))
                xs.append((jnp.where((row_id & m) != 0, q, k) * dec).astype(BF16))
            qb16 = q.astype(BF16)
            kb16 = k.astype(BF16)
            outs = []
            for h in range(nh):
                sl = slice(h * REC_D, (h + 1) * REC_D)
                a = (lax.dot_general(qb16[:, sl], kb16[:, sl], nt, preferred_element_type=F32)
                     * pm_ref[len(REC_LEVELS)])
                for li in range(len(REC_LEVELS)):
                    xh = xs[li][:, sl]
                    a = a + lax.dot_general(xh, xh, nt, preferred_element_type=F32) * pm_ref[li]
                outs.append(head_out(h, a, qg, kg, vb, jnp.exp(bl)))
            store(rs, outs)
            return carry

        lax.fori_loop(0, nchunks, chunk_body, 0)


def _rec_mixer(rf, v, norm_g, tb):
    b, t, dd = v.shape
    nh = dd // REC_D
    sel, pmask = _rec_tables()
    sel = jnp.asarray(sel, BF16)
    pmask = jnp.asarray(pmask, F32)
    tile = pl.BlockSpec((1, tb, dd), lambda i, j: (i, j, 0))
    return pl.pallas_call(
        _rec_kernel,
        out_shape=jax.ShapeDtypeStruct((b, t, dd), BF16),
        grid=(b, t // tb),
        in_specs=[pl.BlockSpec((1, tb, 4 * dd), lambda i, j: (i, j, 0)), tile,
                  pl.BlockSpec((1, REC_D), lambda i, j: (0, 0)),
                  pl.BlockSpec(sel.shape, lambda i, j: (0, 0)),
                  pl.BlockSpec(pmask.shape, lambda i, j: (0, 0, 0))],
        out_specs=tile,
        scratch_shapes=[pltpu.VMEM((nh, REC_D, REC_D), F32)],
        compiler_params=_cparams(("arbitrary", "arbitrary")),
        name="rec_mixer",
    )(rf, v, norm_g.reshape(1, REC_D), sel, pmask)


def _att_kernel(q_ref, *refs):
    nblk = (len(refs) - 3) // 2
    k_refs, v_refs = refs[:nblk], refs[nblk:2 * nblk]
    g_ref, o_ref, bias_ref = refs[2 * nblk:]
    qb = q_ref.shape[1]
    da = q_ref.shape[2]
    hist = (nblk - 1) * qb
    t = pl.program_id(1)

    @pl.when((pl.program_id(0) == 0) & (t == 0))
    def _():
        gw = g_ref.shape[2]
        qrow = lax.broadcasted_iota(jnp.int32, (qb, hist + qb), 0)
        kcol = lax.broadcasted_iota(jnp.int32, (qb, hist + qb), 1)
        dchunk = kcol // CHUNK - qrow // CHUNK
        band = (dchunk >= 0) & (dchunk <= ATT_LEFT)
        for h in range(da // ATT_DH):
            gb = jnp.broadcast_to(g_ref[h], (qb, gw))
            skew = pltpu.roll(gb, gw - (qb - 1), 1, stride=1, stride_axis=0)
            bias_ref[h] = jnp.where(band, skew[:, :hist + qb] * LOG2E, NEG_BIG)

    nh = da // ATT_DH
    lane_head = lax.broadcasted_iota(jnp.int32, (qb, da), 1) // ATT_DH

    def attend(mask_start):
        q = q_ref[0]
        kk = jnp.concatenate([r[0] for r in k_refs], axis=0)
        vv = jnp.concatenate([r[0] for r in v_refs], axis=0)
        scores = [lax.dot_general(jnp.where(lane_head == h, q, jnp.zeros_like(q)), kk,
                                  (((1,), (1,)), ((), ())), preferred_element_type=F32)
                  for h in range(nh)]
        if mask_start:
            kidx = lax.broadcasted_iota(jnp.int32, (qb, hist + qb), 1)
            in_seq = kidx >= hist - t * qb
        out = jnp.zeros((qb, da), F32)
        for h in range(nh):
            s = scores[h] + bias_ref[h]
            if mask_start:
                s = jnp.where(in_seq, s, NEG_BIG)
            e = jnp.exp2(s - jnp.max(s, axis=-1, keepdims=True))
            l = jnp.sum(e, axis=-1, keepdims=True)
            oh = jnp.dot(e.astype(BF16), vv, preferred_element_type=F32)
            out = out + jnp.where(lane_head == h, oh / l, 0.0)
        o_ref[0] = out.astype(o_ref.dtype)

    @pl.when(t * qb < hist)
    def _():
        attend(True)

    @pl.when(t * qb >= hist)
    def _():
        attend(False)


def _att_rel_row(rel_table, qb):
    hist = ATT_LEFT * CHUNK
    gw = pl.next_power_of_2(2 * qb + hist - 1)
    n_lo = qb - 1 + hist - MAX_REL
    n_hi = gw - n_lo - (2 * MAX_REL + 1)
    nh = rel_table.shape[0]
    tab = rel_table.astype(F32)
    row = jnp.concatenate([jnp.broadcast_to(tab[:, :1], (nh, n_lo)), tab,
                           jnp.broadcast_to(tab[:, -1:], (nh, n_hi))], axis=1)
    return row.reshape(nh, 1, gw)


def _att_mixer(q, k, v, rel_table, qb):
    b, t, da = q.shape
    hist = ATT_LEFT * CHUNK
    grow = _att_rel_row(rel_table, qb)
    tile = pl.BlockSpec((1, qb, da), lambda i, j: (i, j, 0))
    back = lambda n: pl.BlockSpec((1, qb, da), lambda i, j: (i, jnp.maximum(j - n, 0), 0))
    blocks = [back(n) for n in range(hist // qb, 0, -1)] + [tile]
    return pl.pallas_call(
        _att_kernel,
        out_shape=jax.ShapeDtypeStruct((b, t, da), BF16),
        grid=(b, t // qb),
        in_specs=[tile] + blocks + blocks + [pl.BlockSpec(grow.shape, lambda i, j: (0, 0, 0))],
        out_specs=tile,
        scratch_shapes=[pltpu.VMEM((ATT_HEADS, qb, hist + qb), F32)],
        compiler_params=_cparams(("arbitrary", "arbitrary")),
        name="att_mixer",
    )(q, *([k] * len(blocks)), *([v] * len(blocks)), grow)


def _tail_kernel(x_ref, yc_ref, yr_ref, ya_ref, wo_ref, g1_ref, l1g_ref, l1b_ref,
                 sc_ref, sh_ref, g2_ref, w1_ref, w2_ref, l2g_ref, l2b_ref, o_ref, a_s, x1_s, h_s,
                 *, alpha, d_ff, fc):
    tm = x_ref.shape[1]
    groups = [slice(i * tm // TAIL_SPLIT, (i + 1) * tm // TAIL_SPLIT) for i in range(TAIL_SPLIT)]
    ys = [jnp.dot(jnp.concatenate([yc_ref[0, rs, :], yr_ref[0, rs, :], ya_ref[0, rs, :]], axis=-1),
                  wo_ref[...], preferred_element_type=F32) for rs in groups]
    for rs, y in zip(groups, ys):
        x1 = _layer_norm(alpha * x_ref[0, rs, :] + (1.0 + g1_ref[0]) * y, l1g_ref[...], l1b_ref[...])
        x1_s[rs, :] = x1
        h_s[rs, :] = (x1 * (1.0 + sc_ref[0]) + sh_ref[0]).astype(BF16)
    for c in range(d_ff // fc):
        for rs in groups:
            gt = jnp.dot(h_s[rs, :], w1_ref[:, c * fc:(c + 1) * fc], preferred_element_type=F32)
            up = jnp.dot(h_s[rs, :], w1_ref[:, d_ff + c * fc:d_ff + (c + 1) * fc],
                         preferred_element_type=F32)
            a_s[rs, c * fc:(c + 1) * fc] = (_silu(gt) * up).astype(BF16)
    y2s = [jnp.dot(a_s[rs, :], w2_ref[...], preferred_element_type=F32) for rs in groups]
    for rs, y2 in zip(groups, y2s):
        o_ref[0, rs, :] = _layer_norm(alpha * x1_s[rs, :] + (1.0 + g2_ref[0]) * y2,
                                      l2g_ref[...], l2b_ref[...])


def _tail(x, yc, yr, ya, wo_bf16, g1, ln1_g, ln1_b, sc2, sh2, g2, w1_bf16, w2_bf16, ln2_g, ln2_b,
          layer, alpha, tm):
    b, t, d = x.shape
    d_ff = w2_bf16.shape[1]
    tile = lambda a: pl.BlockSpec((1, tm, a.shape[2]), lambda i, j: (i, j, 0))
    mod = pl.BlockSpec((1, 1, d), lambda i, j: (i, 0, 0))
    row = pl.BlockSpec((1, d), lambda i, j: (0, 0))
    weight = lambda a: _layer_block(a, layer, pipeline_mode=pl.Buffered(1))
    return pl.pallas_call(
        functools.partial(_tail_kernel, alpha=alpha, d_ff=d_ff, fc=FFN_FC),
        out_shape=jax.ShapeDtypeStruct((b, t, d), F32),
        grid=(b, t // tm),
        in_specs=[tile(x), tile(yc), tile(yr), tile(ya), weight(wo_bf16), mod, row, row,
                  mod, mod, mod, weight(w1_bf16), weight(w2_bf16), row, row],
        out_specs=tile(x),
        scratch_shapes=[pltpu.VMEM((tm, d_ff), BF16),
                        pltpu.VMEM((tm, d), F32),
                        pltpu.VMEM((tm, d), BF16)],
        compiler_params=_cparams(("arbitrary", "arbitrary")),
        name="tail",
    )(x, yc, yr, ya, wo_bf16, g1, ln1_g.reshape(1, d), ln1_b.reshape(1, d), sc2, sh2, g2,
      w1_bf16, w2_bf16, ln2_g.reshape(1, d), ln2_b.reshape(1, d))


def kernel(x, c, w_ada, b_ada, w_in, conv_w, conv_b, conv_ln_g, conv_ln_b, rec_lower_bound,
           rec_norm_g, rel_bias, w_out, ln1_g, ln1_b, ln2_g, ln2_b, w_ffn_in, w_ffn_out):
    depth = w_in.shape[0]
    b, t, d = x.shape
    alpha = (2 * depth) ** 0.25

    lbs = jax.nn.softmax(rec_lower_bound.astype(F32), axis=0)
    lbs = jnp.cumsum(lbs, axis=0) - lbs[0]
    mods = _ada_mod(c, w_ada, b_ada)
    w_in, w_out, w_ffn_in, w_ffn_out = [w.astype(BF16) for w in (w_in, w_out, w_ffn_in, w_ffn_out)]

    for l in range(depth):
        sh1, sc1, g1, sh2, sc2, g2 = [m.reshape(b, 1, d) for m in jnp.split(mods[l], 6, axis=-1)]
        y_conv, rf, ri, aq, ak, av = _front(
            x, sc1, sh1, w_in, l, conv_w[l], conv_b[l], conv_ln_g[l], conv_ln_b[l], lbs[l],
            tm=FRONT_ROWS)
        y_rec = _rec_mixer(rf, ri, rec_norm_g[l], tb=REC_TB)
        y_att = _att_mixer(aq, ak, av, rel_bias[l], qb=ATT_QB)
        x = _tail(x, y_conv, y_rec, y_att, w_out, g1, ln1_g[l], ln1_b[l], sc2, sh2, g2,
                  w_ffn_in, w_ffn_out, ln2_g[l], ln2_b[l], l, alpha, tm=TILE_ROWS)
    return x
```

```python
import functools
import math

import numpy as np
import jax
import jax.numpy as jnp
from jax import lax
from jax.experimental import pallas as pl
from jax.experimental.pallas import tpu as pltpu

F32 = jnp.float32
BF16 = jnp.bfloat16

CHUNK = 64
CONV_WIDTH = 31
REC_HEADS = 4
REC_D = 128
ATT_HEADS = 4
ATT_DH = 64
ATT_LEFT = 8
MAX_REL = 128
LN_EPS = 1e-5
NEG_BIG = -1e30
TINY = 1e-30
LOG2E = math.log2(math.e)

FRONT_ROWS = 512
TILE_ROWS = 1024
ATT_QB = 256
CONV_TAIL = 32
CONV_RS = 32
GATE_RS = 16
FFN_FC = 256
TAIL_SPLIT = 4
SUBLANES = 8

VMEM_LIMIT = 56 * 1024 * 1024


def _cparams(sem):
    return pltpu.CompilerParams(dimension_semantics=sem, vmem_limit_bytes=VMEM_LIMIT)


def _layer_norm(x, g, b):
    mu = jnp.mean(x, axis=-1, keepdims=True)
    xc = x - mu
    var = jnp.mean(xc * xc, axis=-1, keepdims=True)
    return xc * lax.rsqrt(var + LN_EPS) * g + b


def _silu(x):
    return x * jax.nn.sigmoid(x)


def _ada_kernel(c_ref, w_ref, b_ref, o_ref):
    ca = _silu(c_ref[...]).astype(BF16)
    o_ref[0] = jnp.dot(ca, w_ref[0].astype(BF16), preferred_element_type=F32) + b_ref[0]


def _ada_mod(c, w_ada, b_ada):
    depth, d, n6 = w_ada.shape
    b = c.shape[0]
    bp = ((b + 7) // 8) * 8
    cp = jnp.zeros((bp, d), F32).at[:b].set(c)
    tn = 1536
    out = pl.pallas_call(
        _ada_kernel,
        out_shape=jax.ShapeDtypeStruct((depth, bp, n6), F32),
        grid=(depth, n6 // tn),
        in_specs=[pl.BlockSpec((bp, d), lambda l, j: (0, 0)),
                  pl.BlockSpec((1, d, tn), lambda l, j: (l, 0, j)),
                  pl.BlockSpec((1, 1, tn), lambda l, j: (l, 0, j))],
        out_specs=pl.BlockSpec((1, bp, tn), lambda l, j: (l, 0, j)),
        compiler_params=_cparams(("arbitrary", "arbitrary")),
        name="ada_mod",
    )(cp, w_ada, b_ada.reshape(depth, 1, n6))
    return out[:, :b]


def _front_kernel(x_ref, sc_ref, sh_ref, w_ref, cw_ref, cb_ref, cg_ref, cbe_ref, lb_ref,
                  ng_ref, sel_ref, pm_ref,
                  yc_ref, yr_ref, aq_ref, ak_ref, av_ref,
                  shs, h_s, z_s, rf_ref, ri_ref, st_ref):
    rq_ref, lf_ref, rk_ref, rg_ref = _rec_streams(rf_ref)
    tm = x_ref.shape[1]
    dc = cw_ref.shape[1]
    dr = lb_ref.shape[1]
    da = aq_ref.shape[2]
    rec0 = 2 * dc
    att0 = 2 * dc + 4 * dr
    quarter = tm // 4

    def stage(fn):
        fn()

    def proj(c0, n):
        return jnp.dot(h_s[...], w_ref[:, c0:c0 + n], preferred_element_type=F32)

    def paced_zero(dep):
        bits = lax.bitcast_convert_type(dep[0:SUBLANES, 0:128], jnp.uint32)
        zero = lax.shift_right_logical(lax.shift_right_logical(bits, jnp.uint32(16)), jnp.uint32(16))
        zero = lax.bitcast_convert_type(zero, F32)
        return jnp.tile(zero, (CONV_RS // SUBLANES, dc // 128))

    def conv_rows(lo, hi, dep):
        off = CONV_TAIL - (CONV_WIDTH - 1)
        acc = dep
        for r0 in range(lo, hi, CONV_RS):
            acc = paced_zero(acc)
            for j in range(CONV_WIDTH):
                a, s = divmod(off + j, SUBLANES)
                acc = acc + cw_ref[j:j + 1, :] * shs[s, r0 + a * SUBLANES:r0 + a * SUBLANES + CONV_RS, :]
            y = _layer_norm(acc + cb_ref[...], cg_ref[...], cbe_ref[...])
            yc_ref[0, r0:r0 + CONV_RS, :] = _silu(y).astype(BF16)

    def gate_rows(lo, hi):
        lb = lb_ref[...]
        lb_floor = jnp.maximum(lb, TINY)
        one_m_lb = 1.0 - lb
        for r0 in range(lo, hi, GATE_RS):
            z = z_s[r0:r0 + GATE_RS, :]
            e = jnp.exp(-jnp.abs(z))
            r = 1.0 / (1.0 + e)
            pos = z >= 0.0
            lf_ref[0, r0:r0 + GATE_RS, :] = jnp.log(lb_floor + one_m_lb * (jnp.where(pos, 1.0, e) * r))
            rk_ref[0, r0:r0 + GATE_RS, :] = one_m_lb * (jnp.where(pos, e, 1.0) * r)

    @pl.when(pl.program_id(1) == 0)
    def _():
        shs[0, 0:CONV_TAIL, :] = jnp.zeros((CONV_TAIL, dc), F32)

    @pl.when(pl.program_id(1) > 0)
    def _():
        shs[0, 0:CONV_TAIL, :] = shs[0, tm:tm + CONV_TAIL, :]

    h_s[...] = (x_ref[0] * (1.0 + sc_ref[0]) + sh_ref[0]).astype(BF16)

    @stage
    def _():
        pc = proj(0, 2 * dc)
        for r0 in range(0, tm, CONV_RS):
            blk = pc[r0:r0 + CONV_RS, :]
            shs[0, CONV_TAIL + r0:CONV_TAIL + r0 + CONV_RS, :] = (
                blk[:, :dc] * jax.nn.sigmoid(blk[:, dc:]))

    @stage
    def _():
        rq_ref[0] = proj(rec0, dr)
        n_sh = tm + CONV_TAIL - SUBLANES
        for s in range(1, SUBLANES):
            for r0 in range(0, n_sh, CONV_RS):
                n = min(CONV_RS, n_sh - r0)
                shs[s, r0:r0 + n, :] = shs[0, r0 + s:r0 + s + n, :]

    @stage
    def _():
        res = proj(rec0 + dr, dr)
        z_s[...] = res
        conv_rows(0, quarter, res)

    @stage
    def _():
        res = proj(rec0 + 2 * dr, dr)
        ri_ref[0] = res.astype(BF16)
        conv_rows(quarter, 2 * quarter, res)
        gate_rows(0, 2 * quarter)

    @stage
    def _():
        res = proj(rec0 + 3 * dr, dr)
        rg_ref[0] = res
        conv_rows(2 * quarter, 3 * quarter, res)
        gate_rows(2 * quarter, tm)

    @stage
    def _():
        aq_ref[0] = (proj(att0, da) * (LOG2E * ATT_DH ** -0.5)).astype(BF16)
        ak_ref[0] = proj(att0 + da, da).astype(BF16)
        res = proj(att0 + 2 * da, da)
        av_ref[0] = res.astype(BF16)
        conv_rows(3 * quarter, tm, res)

    _rec_body(rf_ref, ri_ref, ng_ref, sel_ref, pm_ref, yr_ref, st_ref)


def _layer_block(stacked, layer, **kw):
    return pl.BlockSpec((None,) + stacked.shape[1:], lambda i, j: (layer, 0, 0), **kw)


def _front(x, sc, sh, w_all_bf16, layer, conv_w, conv_b, conv_ln_g, conv_ln_b, lb, norm_g, tm):
    b, t, d = x.shape
    dc = conv_w.shape[1]
    dr = lb.shape[0]
    da = ATT_HEADS * ATT_DH
    row = lambda a: a.reshape(1, -1)
    const = lambda a: pl.BlockSpec(a.shape, lambda i, j: (0,) * a.ndim)
    mod = pl.BlockSpec((1, 1, d), lambda i, j: (i, 0, 0))
    tile = lambda n: pl.BlockSpec((1, tm, n), lambda i, j: (i, j, 0))
    outs = [(dc, BF16), (dr, BF16), (da, BF16), (da, BF16), (da, BF16)]
    sel, pmask = _rec_tables()
    consts = [conv_w, row(conv_b), row(conv_ln_g), row(conv_ln_b), row(lb), row(norm_g),
              jnp.asarray(sel, BF16), jnp.asarray(pmask, F32)]
    return pl.pallas_call(
        _front_kernel,
        out_shape=[jax.ShapeDtypeStruct((b, t, n), dt) for n, dt in outs],
        grid=(b, t // tm),
        in_specs=[tile(d), mod, mod, _layer_block(w_all_bf16, layer)] + [const(a) for a in consts],
        out_specs=[tile(n) for n, _ in outs],
        scratch_shapes=[pltpu.VMEM((SUBLANES, tm + CONV_TAIL, dc), F32),
                        pltpu.VMEM((tm, d), BF16),
                        pltpu.VMEM((tm, dr), F32),
                        pltpu.VMEM((1, tm, 4 * dr), F32),
                        pltpu.VMEM((1, tm, dr), BF16),
                        pltpu.VMEM((dr // REC_D, REC_D, REC_D), F32)],
        compiler_params=_cparams(("arbitrary", "arbitrary")),
        name="front",
    )(x, sc, sh, w_all_bf16, *consts)


REC_LEVELS = tuple(CHUNK >> (i + 1) for i in range(int(math.log2(CHUNK))))
REC_SAFE_RANGE = 150.0
REC_SPLIT = 2


def _rec_tables():
    x = np.arange(CHUNK)
    sel = [x[:, None] >= x[None, :]]
    masks = []
    for m in REC_LEVELS:
        bnd = (x // (2 * m)) * (2 * m) + m - 1
        sel.append(bnd[:, None] >= x[None, :])
        same = (x[:, None] // (2 * m)) == (x[None, :] // (2 * m))
        masks.append(same & ((x[:, None] & m) != 0) & ((x[None, :] & m) == 0))
    masks.append(x[:, None] == x[None, :])
    sel = np.concatenate(sel, axis=0).astype(np.float32)
    return np.concatenate([sel] * REC_SPLIT, axis=1), np.stack(masks).astype(np.float32)


def _split_bf16(x):
    parts = []
    for _ in range(REC_SPLIT):
        p = x.astype(BF16)
        parts.append(p)
        x = x - p.astype(F32)
    return parts


def _rec_streams(rf_ref):
    dd = rf_ref.shape[2] // 4
    return [rf_ref.at[:, :, i * dd:(i + 1) * dd] for i in range(4)]


def _rec_body(rf_ref, i_ref, ng_ref, sel_ref, pm_ref, o_ref, st_ref):
    q_ref, lf_ref, k_ref, g_ref = _rec_streams(rf_ref)
    tb = q_ref.shape[1]
    dd = q_ref.shape[2]
    nh = dd // REC_D
    nchunks = tb // CHUNK
    nt = (((1,), (1,)), ((), ()))
    tn = (((0,), (0,)), ((), ()))

    @pl.when(pl.program_id(1) == 0)
    def _():
        st_ref[...] = jnp.zeros(st_ref.shape, F32)

    tot = [jnp.sum(lf_ref[0, c * CHUNK:(c + 1) * CHUNK, :], axis=0, keepdims=True)
           for c in range(nchunks)]
    safe = jnp.min(functools.reduce(jnp.minimum, tot)) >= -REC_SAFE_RANGE

    def prefix(lf, nrows):
        return jnp.dot(sel_ref[0:nrows, :], jnp.concatenate(_split_bf16(lf), axis=0),
                       preferred_element_type=F32)

    def head_out(h, a, qg, kg, vb, eb):
        sl = slice(h * REC_D, (h + 1) * REC_D)
        st = st_ref[h]
        o_h = (lax.dot_general(qg[:, sl], st.astype(BF16), nt, preferred_element_type=F32)
               + jnp.dot(a.astype(BF16), vb[:, sl], preferred_element_type=F32))
        upd = lax.dot_general(vb[:, sl], kg[:, sl], tn, preferred_element_type=F32)
        st_ref[h] = st * eb[:, sl] + upd
        ms = jnp.mean(o_h * o_h, axis=-1, keepdims=True)
        return o_h * lax.rsqrt(ms + LN_EPS) * ng_ref[...]

    def store(rs, outs):
        y = jnp.concatenate(outs, axis=-1) * _silu(g_ref[0, rs, :])
        o_ref[0, rs, :] = y.astype(o_ref.dtype)

    @pl.when(safe)
    def _():
        causal = (lax.broadcasted_iota(jnp.int32, (CHUNK, CHUNK), 0)
                  >= lax.broadcasted_iota(jnp.int32, (CHUNK, CHUNK), 1))
        heads = [slice(h * REC_D, (h + 1) * REC_D) for h in range(nh)]
        rows = [slice(c * CHUNK, (c + 1) * CHUNK) for c in range(nchunks)]
        bcs = [prefix(lf_ref[0, rs, :], CHUNK) for rs in rows]
        qg, kg, vb, eb, att, upd = [], [], [], [], [], []
        for rs, bc in zip(rows, bcs):
            half = 0.5 * bc[CHUNK - 1:CHUNK, :]
            eh = jnp.exp(half)
            qf = q_ref[0, rs, :] * jnp.exp(bc - half)
            kf = k_ref[0, rs, :] * jnp.exp(half - bc)
            qi = qf.astype(BF16)
            ki = kf.astype(BF16)
            qg.append((qf * eh).astype(BF16))
            kg.append((kf * eh).astype(BF16))
            vb.append(i_ref[0, rs, :])
            eb.append(eh * eh)
            att.append([jnp.where(causal, lax.dot_general(qi[:, sl], ki[:, sl], nt,
                                                          preferred_element_type=F32),
                                  0.0).astype(BF16) for sl in heads])
        for c in range(nchunks):
            upd.append([lax.dot_general(vb[c][:, sl], kg[c][:, sl], tn, preferred_element_type=F32)
                        for sl in heads])
        states = []
        for h, sl in enumerate(heads):
            st = st_ref[h]
            per_chunk = []
            for c in range(nchunks):
                per_chunk.append(st.astype(BF16))
                st = st * eb[c][:, sl] + upd[c][h]
            st_ref[h] = st
            states.append(per_chunk)
        for c, rs in enumerate(rows):
            outs = []
            for h, sl in enumerate(heads):
                o_h = (lax.dot_general(qg[c][:, sl], states[h][c], nt, preferred_element_type=F32)
                       + jnp.dot(att[c][h], vb[c][:, sl], preferred_element_type=F32))
                ms = jnp.mean(o_h * o_h, axis=-1, keepdims=True)
                outs.append(o_h * lax.rsqrt(ms + LN_EPS) * ng_ref[...])
            store(rs, outs)

    @pl.when(jnp.logical_not(safe))
    def _():
        row_id = lax.broadcasted_iota(jnp.int32, (CHUNK, dd), 0)

        def chunk_body(ci, carry):
            rs = pl.ds(pl.multiple_of(ci * CHUNK, CHUNK), CHUNK)
            q = q_ref[0, rs, :]
            k = k_ref[0, rs, :]
            pre = prefix(lf_ref[0, rs, :], sel_ref.shape[0])
            bc = pre[0:CHUNK, :]
            bl = bc[CHUNK - 1:CHUNK, :]
            qg = (q * jnp.exp(bc)).astype(BF16)
            kg = (k * jnp.exp(bl - bc)).astype(BF16)
            vb = i_ref[0, rs, :]
            xs = []
            for li, m in enumerate(REC_LEVELS):
                dec = jnp.exp(-jnp.abs(bc - pre[(li + 1) * CHUNK:(li + 2) * CHUNK, :]))
                xs.append((jnp.where((row_id & m) != 0, q, k) * dec).astype(BF16))
            qb16 = q.astype(BF16)
            kb16 = k.astype(BF16)
            outs = []
            for h in range(nh):
                sl = slice(h * REC_D, (h + 1) * REC_D)
                a = (lax.dot_general(qb16[:, sl], kb16[:, sl], nt, preferred_element_type=F32)
                     * pm_ref[len(REC_LEVELS)])
                for li in range(len(REC_LEVELS)):
                    xh = xs[li][:, sl]
                    a = a + lax.dot_general(xh, xh, nt, preferred_element_type=F32) * pm_ref[li]
                outs.append(head_out(h, a, qg, kg, vb, jnp.exp(bl)))
            store(rs, outs)
            return carry

        lax.fori_loop(0, nchunks, chunk_body, 0)


def _att_kernel(q_ref, *refs):
    nblk = (len(refs) - 3) // 2
    k_refs, v_refs = refs[:nblk], refs[nblk:2 * nblk]
    g_ref, o_ref, bias_ref = refs[2 * nblk:]
    qb = q_ref.shape[1]
    da = q_ref.shape[2]
    hist = (nblk - 1) * qb
    t = pl.program_id(1)

    @pl.when((pl.program_id(0) == 0) & (t == 0))
    def _():
        gw = g_ref.shape[2]
        qrow = lax.broadcasted_iota(jnp.int32, (qb, hist + qb), 0)
        kcol = lax.broadcasted_iota(jnp.int32, (qb, hist + qb), 1)
        dchunk = kcol // CHUNK - qrow // CHUNK
        band = (dchunk >= 0) & (dchunk <= ATT_LEFT)
        for h in range(da // ATT_DH):
            gb = jnp.broadcast_to(g_ref[h], (qb, gw))
            skew = pltpu.roll(gb, gw - (qb - 1), 1, stride=1, stride_axis=0)
            bias_ref[h] = jnp.where(band, skew[:, :hist + qb] * LOG2E, NEG_BIG)

    nh = da // ATT_DH
    lane_head = lax.broadcasted_iota(jnp.int32, (qb, da), 1) // ATT_DH

    def attend(mask_start):
        q = q_ref[0]
        kk = jnp.concatenate([r[0] for r in k_refs], axis=0)
        vv = jnp.concatenate([r[0] for r in v_refs], axis=0)
        scores = [lax.dot_general(jnp.where(lane_head == h, q, jnp.zeros_like(q)), kk,
                                  (((1,), (1,)), ((), ())), preferred_element_type=F32)
                  for h in range(nh)]
        if mask_start:
            kidx = lax.broadcasted_iota(jnp.int32, (qb, hist + qb), 1)
            in_seq = kidx >= hist - t * qb
        out = jnp.zeros((qb, da), F32)
        for h in range(nh):
            s = scores[h] + bias_ref[h]
            if mask_start:
                s = jnp.where(in_seq, s, NEG_BIG)
            e = jnp.exp2(s - jnp.max(s, axis=-1, keepdims=True))
            l = jnp.sum(e, axis=-1, keepdims=True)
            oh = jnp.dot(e.astype(BF16), vv, preferred_element_type=F32)
            out = out + jnp.where(lane_head == h, oh / l, 0.0)
        o_ref[0] = out.astype(o_ref.dtype)

    @pl.when(t * qb < hist)
    def _():
        attend(True)

    @pl.when(t * qb >= hist)
    def _():
        attend(False)


def _att_rel_row(rel_table, qb):
    hist = ATT_LEFT * CHUNK
    gw = pl.next_power_of_2(2 * qb + hist - 1)
    n_lo = qb - 1 + hist - MAX_REL
    n_hi = gw - n_lo - (2 * MAX_REL + 1)
    nh = rel_table.shape[0]
    tab = rel_table.astype(F32)
    row = jnp.concatenate([jnp.broadcast_to(tab[:, :1], (nh, n_lo)), tab,
                           jnp.broadcast_to(tab[:, -1:], (nh, n_hi))], axis=1)
    return row.reshape(nh, 1, gw)


def _att_mixer(q, k, v, rel_table, qb):
    b, t, da = q.shape
    hist = ATT_LEFT * CHUNK
    grow = _att_rel_row(rel_table, qb)
    tile = pl.BlockSpec((1, qb, da), lambda i, j: (i, j, 0))
    back = lambda n: pl.BlockSpec((1, qb, da), lambda i, j: (i, jnp.maximum(j - n, 0), 0))
    blocks = [back(n) for n in range(hist // qb, 0, -1)] + [tile]
    return pl.pallas_call(
        _att_kernel,
        out_shape=jax.ShapeDtypeStruct((b, t, da), BF16),
        grid=(b, t // qb),
        in_specs=[tile] + blocks + blocks + [pl.BlockSpec(grow.shape, lambda i, j: (0, 0, 0))],
        out_specs=tile,
        scratch_shapes=[pltpu.VMEM((ATT_HEADS, qb, hist + qb), F32)],
        compiler_params=_cparams(("arbitrary", "arbitrary")),
        name="att_mixer",
    )(q, *([k] * len(blocks)), *([v] * len(blocks)), grow)


def _tail_kernel(x_ref, yc_ref, yr_ref, ya_ref, wo_ref, g1_ref, l1g_ref, l1b_ref,
                 sc_ref, sh_ref, g2_ref, w1_ref, w2_ref, l2g_ref, l2b_ref, o_ref, a_s, x1_s, h_s,
                 *, alpha, d_ff, fc):
    tm = x_ref.shape[1]
    groups = [slice(i * tm // TAIL_SPLIT, (i + 1) * tm // TAIL_SPLIT) for i in range(TAIL_SPLIT)]
    ys = [jnp.dot(jnp.concatenate([yc_ref[0, rs, :], yr_ref[0, rs, :], ya_ref[0, rs, :]], axis=-1),
                  wo_ref[...], preferred_element_type=F32) for rs in groups]
    for rs, y in zip(groups, ys):
        x1 = _layer_norm(alpha * x_ref[0, rs, :] + (1.0 + g1_ref[0]) * y, l1g_ref[...], l1b_ref[...])
        x1_s[rs, :] = x1
        h_s[rs, :] = (x1 * (1.0 + sc_ref[0]) + sh_ref[0]).astype(BF16)
    for c in range(d_ff // fc):
        for rs in groups:
            gt = jnp.dot(h_s[rs, :], w1_ref[:, c * fc:(c + 1) * fc], preferred_element_type=F32)
            up = jnp.dot(h_s[rs, :], w1_ref[:, d_ff + c * fc:d_ff + (c + 1) * fc],
                         preferred_element_type=F32)
            a_s[rs, c * fc:(c + 1) * fc] = (_silu(gt) * up).astype(BF16)
    y2s = [jnp.dot(a_s[rs, :], w2_ref[...], preferred_element_type=F32) for rs in groups]
    for rs, y2 in zip(groups, y2s):
        o_ref[0, rs, :] = _layer_norm(alpha * x1_s[rs, :] + (1.0 + g2_ref[0]) * y2,
                                      l2g_ref[...], l2b_ref[...])


def _tail(x, yc, yr, ya, wo_bf16, g1, ln1_g, ln1_b, sc2, sh2, g2, w1_bf16, w2_bf16, ln2_g, ln2_b,
          layer, alpha, tm):
    b, t, d = x.shape
    d_ff = w2_bf16.shape[1]
    tile = lambda a: pl.BlockSpec((1, tm, a.shape[2]), lambda i, j: (i, j, 0))
    mod = pl.BlockSpec((1, 1, d), lambda i, j: (i, 0, 0))
    row = pl.BlockSpec((1, d), lambda i, j: (0, 0))
    weight = lambda a: _layer_block(a, layer, pipeline_mode=pl.Buffered(1))
    return pl.pallas_call(
        functools.partial(_tail_kernel, alpha=alpha, d_ff=d_ff, fc=FFN_FC),
        out_shape=jax.ShapeDtypeStruct((b, t, d), F32),
        grid=(b, t // tm),
        in_specs=[tile(x), tile(yc), tile(yr), tile(ya), weight(wo_bf16), mod, row, row,
                  mod, mod, mod, weight(w1_bf16), weight(w2_bf16), row, row],
        out_specs=tile(x),
        scratch_shapes=[pltpu.VMEM((tm, d_ff), BF16),
                        pltpu.VMEM((tm, d), F32),
                        pltpu.VMEM((tm, d), BF16)],
        compiler_params=_cparams(("arbitrary", "arbitrary")),
        name="tail",
    )(x, yc, yr, ya, wo_bf16, g1, ln1_g.reshape(1, d), ln1_b.reshape(1, d), sc2, sh2, g2,
      w1_bf16, w2_bf16, ln2_g.reshape(1, d), ln2_b.reshape(1, d))


def kernel(x, c, w_ada, b_ada, w_in, conv_w, conv_b, conv_ln_g, conv_ln_b, rec_lower_bound,
           rec_norm_g, rel_bias, w_out, ln1_g, ln1_b, ln2_g, ln2_b, w_ffn_in, w_ffn_out):
    depth = w_in.shape[0]
    b, t, d = x.shape
    alpha = (2 * depth) ** 0.25

    lbs = jax.nn.softmax(rec_lower_bound.astype(F32), axis=0)
    lbs = jnp.cumsum(lbs, axis=0) - lbs[0]
    mods = _ada_mod(c, w_ada, b_ada)
    w_in, w_out, w_ffn_in, w_ffn_out = [w.astype(BF16) for w in (w_in, w_out, w_ffn_in, w_ffn_out)]

    for l in range(depth):
        sh1, sc1, g1, sh2, sc2, g2 = [m.reshape(b, 1, d) for m in jnp.split(mods[l], 6, axis=-1)]
        y_conv, y_rec, aq, ak, av = _front(
            x, sc1, sh1, w_in, l, conv_w[l], conv_b[l], conv_ln_g[l], conv_ln_b[l], lbs[l],
            rec_norm_g[l], tm=FRONT_ROWS)
        y_att = _att_mixer(aq, ak, av, rel_bias[l], qb=ATT_QB)
        x = _tail(x, y_conv, y_rec, y_att, w_out, g1, ln1_g[l], ln1_b[l], sc2, sh2, g2,
                  w_ffn_in, w_ffn_out, ln2_g[l], ln2_b[l], l, alpha, tm=TILE_ROWS)
    return x
```

```python
import functools
import math

import numpy as np
import jax
import jax.numpy as jnp
from jax import lax
from jax.experimental import pallas as pl
from jax.experimental.pallas import tpu as pltpu

F32 = jnp.float32
BF16 = jnp.bfloat16

CHUNK = 64
CONV_WIDTH = 31
REC_HEADS = 4
REC_D = 128
ATT_HEADS = 4
ATT_DH = 64
ATT_LEFT = 8
MAX_REL = 128
LN_EPS = 1e-5
NEG_BIG = -1e30
TINY = 1e-30
LOG2E = math.log2(math.e)

ADA_COLS = 3072
FRONT_ROWS = 512
TILE_ROWS = 1024
ATT_QB = 256
CONV_TAIL = 32
CONV_RS = 32
GATE_RS = 16
FFN_FC = 256
TAIL_SPLIT = 4
SUBLANES = 8

VMEM_LIMIT = 56 * 1024 * 1024


def _cparams(sem):
    return pltpu.CompilerParams(dimension_semantics=sem, vmem_limit_bytes=VMEM_LIMIT)


def _layer_norm(x, g, b):
    mu = jnp.mean(x, axis=-1, keepdims=True)
    xc = x - mu
    var = jnp.mean(xc * xc, axis=-1, keepdims=True)
    return xc * lax.rsqrt(var + LN_EPS) * g + b


def _silu(x):
    return x * jax.nn.sigmoid(x)


def _ada_kernel(c_ref, w_ref, b_ref, o_ref):
    ca = _silu(c_ref[...]).astype(BF16)
    o_ref[0] = jnp.dot(ca, w_ref[0].astype(BF16), preferred_element_type=F32) + b_ref[0]


def _ada_mod(c, w_ada, b_ada):
    depth, d, n6 = w_ada.shape
    b = c.shape[0]
    bp = -(-b // SUBLANES) * SUBLANES
    cp = jnp.zeros((bp, d), F32).at[:b].set(c)
    tn = ADA_COLS
    out = pl.pallas_call(
        _ada_kernel,
        out_shape=jax.ShapeDtypeStruct((depth, bp, n6), F32),
        grid=(depth, n6 // tn),
        in_specs=[pl.BlockSpec((bp, d), lambda l, j: (0, 0)),
                  pl.BlockSpec((1, d, tn), lambda l, j: (l, 0, j)),
                  pl.BlockSpec((1, 1, tn), lambda l, j: (l, 0, j))],
        out_specs=pl.BlockSpec((1, bp, tn), lambda l, j: (l, 0, j)),
        compiler_params=_cparams(("arbitrary", "arbitrary")),
        name="ada_mod",
    )(cp, w_ada, b_ada.reshape(depth, 1, n6))
    return out[:, :b]


def _front_kernel(x_ref, sc_ref, sh_ref, w_ref, cw_ref, cb_ref, cg_ref, cbe_ref, lb_ref,
                  ng_ref, sel_ref, pm_ref,
                  yc_ref, yr_ref, aq_ref, ak_ref, av_ref,
                  shs, h_s, z_s, rf_ref, ri_ref, st_ref):
    rq_ref, lf_ref, rk_ref, rg_ref = _rec_streams(rf_ref)
    tm = x_ref.shape[1]
    dc = cw_ref.shape[1]
    dr = lb_ref.shape[1]
    da = aq_ref.shape[2]
    rec0 = 2 * dc
    att0 = 2 * dc + 4 * dr
    quarter = tm // 4

    def stage(fn):
        fn()

    def proj(c0, n):
        return jnp.dot(h_s[...], w_ref[:, c0:c0 + n], preferred_element_type=F32)

    def paced_zero(dep):
        bits = lax.bitcast_convert_type(dep[0:SUBLANES, 0:128], jnp.uint32)
        zero = lax.shift_right_logical(lax.shift_right_logical(bits, jnp.uint32(16)), jnp.uint32(16))
        zero = lax.bitcast_convert_type(zero, F32)
        return jnp.tile(zero, (CONV_RS // SUBLANES, dc // 128))

    def conv_rows(lo, hi, dep):
        off = CONV_TAIL - (CONV_WIDTH - 1)
        acc = dep
        for r0 in range(lo, hi, CONV_RS):
            acc = paced_zero(acc)
            for j in range(CONV_WIDTH):
                a, s = divmod(off + j, SUBLANES)
                acc = acc + cw_ref[j:j + 1, :] * shs[s, r0 + a * SUBLANES:r0 + a * SUBLANES + CONV_RS, :]
            y = _layer_norm(acc + cb_ref[...], cg_ref[...], cbe_ref[...])
            yc_ref[0, r0:r0 + CONV_RS, :] = _silu(y).astype(BF16)

    def gate_rows(lo, hi):
        lb = lb_ref[...]
        lb_floor = jnp.maximum(lb, TINY)
        one_m_lb = 1.0 - lb
        for r0 in range(lo, hi, GATE_RS):
            z = z_s[r0:r0 + GATE_RS, :]
            e = jnp.exp(-jnp.abs(z))
            r = 1.0 / (1.0 + e)
            pos = z >= 0.0
            lf_ref[0, r0:r0 + GATE_RS, :] = jnp.log(lb_floor + one_m_lb * (jnp.where(pos, 1.0, e) * r))
            rk_ref[0, r0:r0 + GATE_RS, :] = one_m_lb * (jnp.where(pos, e, 1.0) * r)

    @pl.when(pl.program_id(1) == 0)
    def _():
        shs[0, 0:CONV_TAIL, :] = jnp.zeros((CONV_TAIL, dc), F32)

    @pl.when(pl.program_id(1) > 0)
    def _():
        shs[0, 0:CONV_TAIL, :] = shs[0, tm:tm + CONV_TAIL, :]

    h_s[...] = (x_ref[0] * (1.0 + sc_ref[0]) + sh_ref[0]).astype(BF16)

    @stage
    def _():
        pc = proj(0, 2 * dc)
        for r0 in range(0, tm, CONV_RS):
            blk = pc[r0:r0 + CONV_RS, :]
            shs[0, CONV_TAIL + r0:CONV_TAIL + r0 + CONV_RS, :] = (
                blk[:, :dc] * jax.nn.sigmoid(blk[:, dc:]))

    @stage
    def _():
        rq_ref[0] = proj(rec0, dr)
        n_sh = tm + CONV_TAIL - SUBLANES
        for s in range(1, SUBLANES):
            for r0 in range(0, n_sh, CONV_RS):
                n = min(CONV_RS, n_sh - r0)
                shs[s, r0:r0 + n, :] = shs[0, r0 + s:r0 + s + n, :]

    @stage
    def _():
        res = proj(rec0 + dr, dr)
        z_s[...] = res
        conv_rows(0, quarter, res)

    @stage
    def _():
        res = proj(rec0 + 2 * dr, dr)
        ri_ref[0] = res.astype(BF16)
        conv_rows(quarter, 2 * quarter, res)
        gate_rows(0, 2 * quarter)

    @stage
    def _():
        res = proj(rec0 + 3 * dr, dr)
        rg_ref[0] = res
        conv_rows(2 * quarter, 3 * quarter, res)
        gate_rows(2 * quarter, tm)

    @stage
    def _():
        aq_ref[0] = (proj(att0, da) * (LOG2E * ATT_DH ** -0.5)).astype(BF16)
        ak_ref[0] = proj(att0 + da, da).astype(BF16)
        res = proj(att0 + 2 * da, da)
        av_ref[0] = res.astype(BF16)
        conv_rows(3 * quarter, tm, res)

    _rec_body(rf_ref, ri_ref, ng_ref, sel_ref, pm_ref, yr_ref, st_ref)


def _layer_block(stacked, layer, **kw):
    return pl.BlockSpec((None,) + stacked.shape[1:], lambda i, j: (layer, 0, 0), **kw)


def _front(x, sc, sh, w_all_bf16, layer, conv_w, conv_b, conv_ln_g, conv_ln_b, lb, norm_g, tm):
    b, t, d = x.shape
    dc = conv_w.shape[1]
    dr = lb.shape[0]
    da = ATT_HEADS * ATT_DH
    row = lambda a: a.reshape(1, -1)
    const = lambda a: pl.BlockSpec(a.shape, lambda i, j: (0,) * a.ndim)
    mod = pl.BlockSpec((1, 1, d), lambda i, j: (i, 0, 0))
    tile = lambda n: pl.BlockSpec((1, tm, n), lambda i, j: (i, j, 0))
    outs = [(dc, BF16), (dr, BF16), (da, BF16), (da, BF16), (da, BF16)]
    sel, pmask = _rec_tables()
    consts = [conv_w, row(conv_b), row(conv_ln_g), row(conv_ln_b), row(lb), row(norm_g),
              jnp.asarray(sel, BF16), jnp.asarray(pmask, F32)]
    return pl.pallas_call(
        _front_kernel,
        out_shape=[jax.ShapeDtypeStruct((b, t, n), dt) for n, dt in outs],
        grid=(b, t // tm),
        in_specs=[tile(d), mod, mod, _layer_block(w_all_bf16, layer)] + [const(a) for a in consts],
        out_specs=[tile(n) for n, _ in outs],
        scratch_shapes=[pltpu.VMEM((SUBLANES, tm + CONV_TAIL, dc), F32),
                        pltpu.VMEM((tm, d), BF16),
                        pltpu.VMEM((tm, dr), F32),
                        pltpu.VMEM((1, tm, 4 * dr), F32),
                        pltpu.VMEM((1, tm, dr), BF16),
                        pltpu.VMEM((dr // REC_D, REC_D, REC_D), F32)],
        compiler_params=_cparams(("arbitrary", "arbitrary")),
        name="front",
    )(x, sc, sh, w_all_bf16, *consts)


REC_LEVELS = tuple(CHUNK >> (i + 1) for i in range(int(math.log2(CHUNK))))
REC_SAFE_RANGE = 150.0
REC_SPLIT = 2


def _rec_tables():
    x = np.arange(CHUNK)
    sel = [x[:, None] >= x[None, :]]
    masks = []
    for m in REC_LEVELS:
        bnd = (x // (2 * m)) * (2 * m) + m - 1
        sel.append(bnd[:, None] >= x[None, :])
        same = (x[:, None] // (2 * m)) == (x[None, :] // (2 * m))
        masks.append(same & ((x[:, None] & m) != 0) & ((x[None, :] & m) == 0))
    masks.append(x[:, None] == x[None, :])
    sel = np.concatenate(sel, axis=0).astype(np.float32)
    return np.concatenate([sel] * REC_SPLIT, axis=1), np.stack(masks).astype(np.float32)


def _split_bf16(x):
    parts = []
    for _ in range(REC_SPLIT):
        p = x.astype(BF16)
        parts.append(p)
        x = x - p.astype(F32)
    return parts


def _rec_streams(rf_ref):
    dd = rf_ref.shape[2] // 4
    return [rf_ref.at[:, :, i * dd:(i + 1) * dd] for i in range(4)]


def _rec_body(rf_ref, i_ref, ng_ref, sel_ref, pm_ref, o_ref, st_ref):
    q_ref, lf_ref, k_ref, g_ref = _rec_streams(rf_ref)
    tb = q_ref.shape[1]
    dd = q_ref.shape[2]
    nh = dd // REC_D
    nchunks = tb // CHUNK
    nt = (((1,), (1,)), ((), ()))
    tn = (((0,), (0,)), ((), ()))

    @pl.when(pl.program_id(1) == 0)
    def _():
        st_ref[...] = jnp.zeros(st_ref.shape, F32)

    tot = [jnp.sum(lf_ref[0, c * CHUNK:(c + 1) * CHUNK, :], axis=0, keepdims=True)
           for c in range(nchunks)]
    safe = jnp.min(functools.reduce(jnp.minimum, tot)) >= -REC_SAFE_RANGE

    def prefix(lf, nrows):
        return jnp.dot(sel_ref[0:nrows, :], jnp.concatenate(_split_bf16(lf), axis=0),
                       preferred_element_type=F32)

    def head_out(h, a, qg, kg, vb, eb):
        sl = slice(h * REC_D, (h + 1) * REC_D)
        st = st_ref[h]
        o_h = (lax.dot_general(qg[:, sl], st.astype(BF16), nt, preferred_element_type=F32)
               + jnp.dot(a.astype(BF16), vb[:, sl], preferred_element_type=F32))
        upd = lax.dot_general(vb[:, sl], kg[:, sl], tn, preferred_element_type=F32)
        st_ref[h] = st * eb[:, sl] + upd
        ms = jnp.mean(o_h * o_h, axis=-1, keepdims=True)
        return o_h * lax.rsqrt(ms + LN_EPS) * ng_ref[...]

    def store(rs, outs):
        y = jnp.concatenate(outs, axis=-1) * _silu(g_ref[0, rs, :])
        o_ref[0, rs, :] = y.astype(o_ref.dtype)

    @pl.when(safe)
    def _():
        causal = (lax.broadcasted_iota(jnp.int32, (CHUNK, CHUNK), 0)
                  >= lax.broadcasted_iota(jnp.int32, (CHUNK, CHUNK), 1))
        heads = [slice(h * REC_D, (h + 1) * REC_D) for h in range(nh)]
        rows = [slice(c * CHUNK, (c + 1) * CHUNK) for c in range(nchunks)]
        bcs = [prefix(lf_ref[0, rs, :], CHUNK) for rs in rows]
        qg, kg, vb, eb, att, upd = [], [], [], [], [], []
        for rs, bc in zip(rows, bcs):
            half = 0.5 * bc[CHUNK - 1:CHUNK, :]
            eh = jnp.exp(half)
            qf = q_ref[0, rs, :] * jnp.exp(bc - half)
            kf = k_ref[0, rs, :] * jnp.exp(half - bc)
            qi = qf.astype(BF16)
            ki = kf.astype(BF16)
            qg.append((qf * eh).astype(BF16))
            kg.append((kf * eh).astype(BF16))
            vb.append(i_ref[0, rs, :])
            eb.append(eh * eh)
            att.append([jnp.where(causal, lax.dot_general(qi[:, sl], ki[:, sl], nt,
                                                          preferred_element_type=F32),
                                  0.0).astype(BF16) for sl in heads])
        for c in range(nchunks):
            upd.append([lax.dot_general(vb[c][:, sl], kg[c][:, sl], tn, preferred_element_type=F32)
                        for sl in heads])
        states = []
        for h, sl in enumerate(heads):
            st = st_ref[h]
            per_chunk = []
            for c in range(nchunks):
                per_chunk.append(st.astype(BF16))
                st = st * eb[c][:, sl] + upd[c][h]
            st_ref[h] = st
            states.append(per_chunk)
        for c, rs in enumerate(rows):
            outs = []
            for h, sl in enumerate(heads):
                o_h = (lax.dot_general(qg[c][:, sl], states[h][c], nt, preferred_element_type=F32)
                       + jnp.dot(att[c][h], vb[c][:, sl], preferred_element_type=F32))
                ms = jnp.mean(o_h * o_h, axis=-1, keepdims=True)
                outs.append(o_h * lax.rsqrt(ms + LN_EPS) * ng_ref[...])
            store(rs, outs)

    @pl.when(jnp.logical_not(safe))
    def _():
        row_id = lax.broadcasted_iota(jnp.int32, (CHUNK, dd), 0)

        def chunk_body(ci, carry):
            rs = pl.ds(pl.multiple_of(ci * CHUNK, CHUNK), CHUNK)
            q = q_ref[0, rs, :]
            k = k_ref[0, rs, :]
            pre = prefix(lf_ref[0, rs, :], sel_ref.shape[0])
            bc = pre[0:CHUNK, :]
            bl = bc[CHUNK - 1:CHUNK, :]
            qg = (q * jnp.exp(bc)).astype(BF16)
            kg = (k * jnp.exp(bl - bc)).astype(BF16)
            vb = i_ref[0, rs, :]
            xs = []
            for li, m in enumerate(REC_LEVELS):
                dec = jnp.exp(-jnp.abs(bc - pre[(li + 1) * CHUNK:(li + 2) * CHUNK, :]))
                xs.append((jnp.where((row_id & m) != 0, q, k) * dec).astype(BF16))
            qb16 = q.astype(BF16)
            kb16 = k.astype(BF16)
            outs = []
            for h in range(nh):
                sl = slice(h * REC_D, (h + 1) * REC_D)
                a = (lax.dot_general(qb16[:, sl], kb16[:, sl], nt, preferred_element_type=F32)
                     * pm_ref[len(REC_LEVELS)])
                for li in range(len(REC_LEVELS)):
                    xh = xs[li][:, sl]
                    a = a + lax.dot_general(xh, xh, nt, preferred_element_type=F32) * pm_ref[li]
                outs.append(head_out(h, a, qg, kg, vb, jnp.exp(bl)))
            store(rs, outs)
            return carry

        lax.fori_loop(0, nchunks, chunk_body, 0)


def _att_kernel(q_ref, *refs):
    nblk = (len(refs) - 3) // 2
    k_refs, v_refs = refs[:nblk], refs[nblk:2 * nblk]
    g_ref, o_ref, bias_ref = refs[2 * nblk:]
    qb = q_ref.shape[1]
    da = q_ref.shape[2]
    hist = (nblk - 1) * qb
    t = pl.program_id(1)

    @pl.when((pl.program_id(0) == 0) & (t == 0))
    def _():
        gw = g_ref.shape[2]
        qrow = lax.broadcasted_iota(jnp.int32, (qb, hist + qb), 0)
        kcol = lax.broadcasted_iota(jnp.int32, (qb, hist + qb), 1)
        dchunk = kcol // CHUNK - qrow // CHUNK
        band = (dchunk >= 0) & (dchunk <= ATT_LEFT)
        for h in range(da // ATT_DH):
            gb = jnp.broadcast_to(g_ref[h], (qb, gw))
            skew = pltpu.roll(gb, gw - (qb - 1), 1, stride=1, stride_axis=0)
            bias_ref[h] = jnp.where(band, skew[:, :hist + qb] * LOG2E, NEG_BIG)

    nh = da // ATT_DH
    lane_head = lax.broadcasted_iota(jnp.int32, (qb, da), 1) // ATT_DH

    def attend(mask_start):
        q = q_ref[0]
        kk = jnp.concatenate([r[0] for r in k_refs], axis=0)
        vv = jnp.concatenate([r[0] for r in v_refs], axis=0)
        scores = [lax.dot_general(jnp.where(lane_head == h, q, jnp.zeros_like(q)), kk,
                                  (((1,), (1,)), ((), ())), preferred_element_type=F32)
                  for h in range(nh)]
        if mask_start:
            kidx = lax.broadcasted_iota(jnp.int32, (qb, hist + qb), 1)
            in_seq = kidx >= hist - t * qb
        out = jnp.zeros((qb, da), F32)
        for h in range(nh):
            s = scores[h] + bias_ref[h]
            if mask_start:
                s = jnp.where(in_seq, s, NEG_BIG)
            e = jnp.exp2(s - jnp.max(s, axis=-1, keepdims=True))
            l = jnp.sum(e, axis=-1, keepdims=True)
            oh = jnp.dot(e.astype(BF16), vv, preferred_element_type=F32)
            out = out + jnp.where(lane_head == h, oh / l, 0.0)
        o_ref[0] = out.astype(o_ref.dtype)

    @pl.when(t * qb < hist)
    def _():
        attend(True)

    @pl.when(t * qb >= hist)
    def _():
        attend(False)


def _att_rel_row(rel_table, qb):
    hist = ATT_LEFT * CHUNK
    gw = pl.next_power_of_2(2 * qb + hist - 1)
    n_lo = qb - 1 + hist - MAX_REL
    n_hi = gw - n_lo - (2 * MAX_REL + 1)
    nh = rel_table.shape[0]
    tab = rel_table.astype(F32)
    row = jnp.concatenate([jnp.broadcast_to(tab[:, :1], (nh, n_lo)), tab,
                           jnp.broadcast_to(tab[:, -1:], (nh, n_hi))], axis=1)
    return row.reshape(nh, 1, gw)


def _att_mixer(q, k, v, rel_table, qb):
    b, t, da = q.shape
    hist = ATT_LEFT * CHUNK
    grow = _att_rel_row(rel_table, qb)
    tile = pl.BlockSpec((1, qb, da), lambda i, j: (i, j, 0))
    back = lambda n: pl.BlockSpec((1, qb, da), lambda i, j: (i, jnp.maximum(j - n, 0), 0))
    blocks = [back(n) for n in range(hist // qb, 0, -1)] + [tile]
    return pl.pallas_call(
        _att_kernel,
        out_shape=jax.ShapeDtypeStruct((b, t, da), BF16),
        grid=(b, t // qb),
        in_specs=[tile] + blocks + blocks + [pl.BlockSpec(grow.shape, lambda i, j: (0, 0, 0))],
        out_specs=tile,
        scratch_shapes=[pltpu.VMEM((ATT_HEADS, qb, hist + qb), F32)],
        compiler_params=_cparams(("arbitrary", "arbitrary")),
        name="att_mixer",
    )(q, *([k] * len(blocks)), *([v] * len(blocks)), grow)


def _tail_kernel(x_ref, yc_ref, yr_ref, ya_ref, wo_ref, g1_ref, l1g_ref, l1b_ref,
                 sc_ref, sh_ref, g2_ref, w1_ref, w2_ref, l2g_ref, l2b_ref, o_ref, a_s, x1_s, h_s,
                 *, alpha, d_ff, fc):
    tm = x_ref.shape[1]
    groups = [slice(i * tm // TAIL_SPLIT, (i + 1) * tm // TAIL_SPLIT) for i in range(TAIL_SPLIT)]
    ys = [jnp.dot(jnp.concatenate([yc_ref[0, rs, :], yr_ref[0, rs, :], ya_ref[0, rs, :]], axis=-1),
                  wo_ref[...], preferred_element_type=F32) for rs in groups]
    for rs, y in zip(groups, ys):
        x1 = _layer_norm(alpha * x_ref[0, rs, :] + (1.0 + g1_ref[0]) * y, l1g_ref[...], l1b_ref[...])
        x1_s[rs, :] = x1
        h_s[rs, :] = (x1 * (1.0 + sc_ref[0]) + sh_ref[0]).astype(BF16)
    for c in range(d_ff // fc):
        for rs in groups:
            gt = jnp.dot(h_s[rs, :], w1_ref[:, c * fc:(c + 1) * fc], preferred_element_type=F32)
            up = jnp.dot(h_s[rs, :], w1_ref[:, d_ff + c * fc:d_ff + (c + 1) * fc],
                         preferred_element_type=F32)
            a_s[rs, c * fc:(c + 1) * fc] = (_silu(gt) * up).astype(BF16)
    y2s = [jnp.dot(a_s[rs, :], w2_ref[...], preferred_element_type=F32) for rs in groups]
    for rs, y2 in zip(groups, y2s):
        o_ref[0, rs, :] = _layer_norm(alpha * x1_s[rs, :] + (1.0 + g2_ref[0]) * y2,
                                      l2g_ref[...], l2b_ref[...])


def _tail(x, yc, yr, ya, wo_bf16, g1, ln1_g, ln1_b, sc2, sh2, g2, w1_bf16, w2_bf16, ln2_g, ln2_b,
          layer, alpha, tm):
    b, t, d = x.shape
    d_ff = w2_bf16.shape[1]
    tile = lambda a: pl.BlockSpec((1, tm, a.shape[2]), lambda i, j: (i, j, 0))
    mod = pl.BlockSpec((1, 1, d), lambda i, j: (i, 0, 0))
    row = pl.BlockSpec((1, d), lambda i, j: (0, 0))
    weight = lambda a: _layer_block(a, layer, pipeline_mode=pl.Buffered(1))
    return pl.pallas_call(
        functools.partial(_tail_kernel, alpha=alpha, d_ff=d_ff, fc=FFN_FC),
        out_shape=jax.ShapeDtypeStruct((b, t, d), F32),
        grid=(b, t // tm),
        in_specs=[tile(x), tile(yc), tile(yr), tile(ya), weight(wo_bf16), mod, row, row,
                  mod, mod, mod, weight(w1_bf16), weight(w2_bf16), row, row],
        out_specs=tile(x),
        scratch_shapes=[pltpu.VMEM((tm, d_ff), BF16),
                        pltpu.VMEM((tm, d), F32),
                        pltpu.VMEM((tm, d), BF16)],
        compiler_params=_cparams(("arbitrary", "arbitrary")),
        name="tail",
    )(x, yc, yr, ya, wo_bf16, g1, ln1_g.reshape(1, d), ln1_b.reshape(1, d), sc2, sh2, g2,
      w1_bf16, w2_bf16, ln2_g.reshape(1, d), ln2_b.reshape(1, d))


def kernel(x, c, w_ada, b_ada, w_in, conv_w, conv_b, conv_ln_g, conv_ln_b, rec_lower_bound,
           rec_norm_g, rel_bias, w_out, ln1_g, ln1_b, ln2_g, ln2_b, w_ffn_in, w_ffn_out):
    depth = w_in.shape[0]
    b, t, d = x.shape
    alpha = (2 * depth) ** 0.25

    lbs = jax.nn.softmax(rec_lower_bound.astype(F32), axis=0)
    lbs = jnp.cumsum(lbs, axis=0) - lbs[0]
    mods = _ada_mod(c, w_ada, b_ada)
    w_in, w_out, w_ffn_in, w_ffn_out = [w.astype(BF16) for w in (w_in, w_out, w_ffn_in, w_ffn_out)]

    for l in range(depth):
        sh1, sc1, g1, sh2, sc2, g2 = [m.reshape(b, 1, d) for m in jnp.split(mods[l], 6, axis=-1)]
        y_conv, y_rec, aq, ak, av = _front(
            x, sc1, sh1, w_in, l, conv_w[l], conv_b[l], conv_ln_g[l], conv_ln_b[l], lbs[l],
            rec_norm_g[l], tm=FRONT_ROWS)
        y_att = _att_mixer(aq, ak, av, rel_bias[l], qb=ATT_QB)
        x = _tail(x, y_conv, y_rec, y_att, w_out, g1, ln1_g[l], ln1_b[l], sc2, sh2, g2,
                  w_ffn_in, w_ffn_out, ln2_g[l], ln2_b[l], l, alpha, tm=TILE_ROWS)
    return x
```

```python
import functools
import math

import numpy as np
import jax
import jax.numpy as jnp
from jax import lax
from jax.experimental import pallas as pl
from jax.experimental.pallas import tpu as pltpu

F32 = jnp.float32
BF16 = jnp.bfloat16

CHUNK = 64
CONV_WIDTH = 31
REC_HEADS = 4
REC_D = 128
ATT_HEADS = 4
ATT_DH = 64
ATT_LEFT = 8
MAX_REL = 128
LN_EPS = 1e-5
NEG_BIG = -1e30
TINY = 1e-30
LOG2E = math.log2(math.e)

ADA_COLS = 3072
FRONT_ROWS = 512
TILE_ROWS = 1024
ATT_QB = 256
CONV_TAIL = 32
CONV_RS = 32
GATE_RS = 16
FFN_FC = 256
TAIL_SPLIT = 4
SUBLANES = 8

VMEM_LIMIT = 56 * 1024 * 1024


def _cparams(sem):
    return pltpu.CompilerParams(dimension_semantics=sem, vmem_limit_bytes=VMEM_LIMIT)


def _layer_norm(x, g, b):
    mu = jnp.mean(x, axis=-1, keepdims=True)
    xc = x - mu
    var = jnp.mean(xc * xc, axis=-1, keepdims=True)
    return xc * lax.rsqrt(var + LN_EPS) * g + b


def _silu(x):
    return x * jax.nn.sigmoid(x)


def _ada_kernel(c_ref, w_ref, b_ref, o_ref):
    ca = _silu(c_ref[...]).astype(BF16)
    o_ref[0] = jnp.dot(ca, w_ref[0].astype(BF16), preferred_element_type=F32) + b_ref[0]


def _ada_mod(c, w_ada, b_ada):
    depth, d, n6 = w_ada.shape
    b = c.shape[0]
    bp = -(-b // SUBLANES) * SUBLANES
    cp = jnp.zeros((bp, d), F32).at[:b].set(c)
    tn = ADA_COLS
    out = pl.pallas_call(
        _ada_kernel,
        out_shape=jax.ShapeDtypeStruct((depth, bp, n6), F32),
        grid=(depth, n6 // tn),
        in_specs=[pl.BlockSpec((bp, d), lambda l, j: (0, 0)),
                  pl.BlockSpec((1, d, tn), lambda l, j: (l, 0, j)),
                  pl.BlockSpec((1, 1, tn), lambda l, j: (l, 0, j))],
        out_specs=pl.BlockSpec((1, bp, tn), lambda l, j: (l, 0, j)),
        compiler_params=_cparams(("arbitrary", "arbitrary")),
        name="ada_mod",
    )(cp, w_ada, b_ada.reshape(depth, 1, n6))
    return out[:, :b]


def _front_kernel(x_ref, sc_ref, sh_ref, w_ref, cw_ref, cb_ref, cg_ref, cbe_ref, lb_ref,
                  ng_ref, sel_ref, pm_ref,
                  yc_ref, yr_ref, aq_ref, ak_ref, av_ref,
                  shs, h_s, z_s, rf_ref, ri_ref, st_ref):
    rq_ref, lf_ref, rk_ref, rg_ref = _rec_streams(rf_ref)
    tm = x_ref.shape[1]
    dc = cw_ref.shape[1]
    dr = lb_ref.shape[1]
    da = aq_ref.shape[2]
    rec0 = 2 * dc
    att0 = 2 * dc + 4 * dr
    quarter = tm // 4

    def stage(fn):
        fn()

    def proj(c0, n):
        return jnp.dot(h_s[...], w_ref[:, c0:c0 + n], preferred_element_type=F32)

    def paced_zero(dep):
        bits = lax.bitcast_convert_type(dep[0:SUBLANES, 0:128], jnp.uint32)
        zero = lax.shift_right_logical(lax.shift_right_logical(bits, jnp.uint32(16)), jnp.uint32(16))
        zero = lax.bitcast_convert_type(zero, F32)
        return jnp.tile(zero, (CONV_RS // SUBLANES, dc // 128))

    def conv_rows(lo, hi, dep):
        off = CONV_TAIL - (CONV_WIDTH - 1)
        acc = dep
        for r0 in range(lo, hi, CONV_RS):
            acc = paced_zero(acc)
            for j in range(CONV_WIDTH):
                a, s = divmod(off + j, SUBLANES)
                acc = acc + cw_ref[j:j + 1, :] * shs[s, r0 + a * SUBLANES:r0 + a * SUBLANES + CONV_RS, :]
            y = _layer_norm(acc + cb_ref[...], cg_ref[...], cbe_ref[...])
            yc_ref[0, r0:r0 + CONV_RS, :] = _silu(y).astype(BF16)

    def gate_rows(lo, hi):
        lb = lb_ref[...]
        lb_floor = jnp.maximum(lb, TINY)
        one_m_lb = 1.0 - lb
        for r0 in range(lo, hi, GATE_RS):
            z = z_s[r0:r0 + GATE_RS, :]
            e = jnp.exp(-jnp.abs(z))
            r = 1.0 / (1.0 + e)
            pos = z >= 0.0
            keep = one_m_lb * (jnp.where(pos, 1.0, e) * r)
            lf_ref[0, r0:r0 + GATE_RS, :] = jnp.log(lb_floor + keep)
            rk_ref[0, r0:r0 + GATE_RS, :] = one_m_lb - keep

    @pl.when(pl.program_id(1) == 0)
    def _():
        shs[0, 0:CONV_TAIL, :] = jnp.zeros((CONV_TAIL, dc), F32)

    @pl.when(pl.program_id(1) > 0)
    def _():
        shs[0, 0:CONV_TAIL, :] = shs[0, tm:tm + CONV_TAIL, :]

    h_s[...] = (x_ref[0] * (1.0 + sc_ref[0]) + sh_ref[0]).astype(BF16)

    @stage
    def _():
        pc = proj(0, 2 * dc)
        for r0 in range(0, tm, CONV_RS):
            blk = pc[r0:r0 + CONV_RS, :]
            shs[0, CONV_TAIL + r0:CONV_TAIL + r0 + CONV_RS, :] = (
                blk[:, :dc] * jax.nn.sigmoid(blk[:, dc:]))

    @stage
    def _():
        rq_ref[0] = proj(rec0, dr)
        n_sh = tm + CONV_TAIL - SUBLANES
        for s in range(1, SUBLANES):
            for r0 in range(0, n_sh, CONV_RS):
                n = min(CONV_RS, n_sh - r0)
                shs[s, r0:r0 + n, :] = shs[0, r0 + s:r0 + s + n, :]

    @stage
    def _():
        res = proj(rec0 + dr, dr)
        z_s[...] = res
        conv_rows(0, quarter, res)

    @stage
    def _():
        res = proj(rec0 + 2 * dr, dr)
        ri_ref[0] = res.astype(BF16)
        conv_rows(quarter, 2 * quarter, res)
        gate_rows(0, 2 * quarter)

    @stage
    def _():
        res = proj(rec0 + 3 * dr, dr)
        rg_ref[0] = res
        conv_rows(2 * quarter, 3 * quarter, res)
        gate_rows(2 * quarter, tm)

    @stage
    def _():
        aq_ref[0] = (proj(att0, da) * (LOG2E * ATT_DH ** -0.5)).astype(BF16)
        ak_ref[0] = proj(att0 + da, da).astype(BF16)
        res = proj(att0 + 2 * da, da)
        av_ref[0] = res.astype(BF16)
        conv_rows(3 * quarter, tm, res)

    _rec_body(rf_ref, ri_ref, ng_ref, sel_ref, pm_ref, yr_ref, st_ref)


def _layer_block(stacked, layer, **kw):
    return pl.BlockSpec((None,) + stacked.shape[1:], lambda i, j: (layer, 0, 0), **kw)


def _front(x, sc, sh, w_all_bf16, layer, conv_w, conv_b, conv_ln_g, conv_ln_b, lb, norm_g, tm):
    b, t, d = x.shape
    dc = conv_w.shape[1]
    dr = lb.shape[0]
    da = ATT_HEADS * ATT_DH
    row = lambda a: a.reshape(1, -1)
    const = lambda a: pl.BlockSpec(a.shape, lambda i, j: (0,) * a.ndim)
    mod = pl.BlockSpec((1, 1, d), lambda i, j: (i, 0, 0))
    tile = lambda n: pl.BlockSpec((1, tm, n), lambda i, j: (i, j, 0))
    outs = [(dc, BF16), (dr, BF16), (da, BF16), (da, BF16), (da, BF16)]
    sel, pmask = _rec_tables()
    consts = [conv_w, row(conv_b), row(conv_ln_g), row(conv_ln_b), row(lb), row(norm_g),
              jnp.asarray(sel, BF16), jnp.asarray(pmask, F32)]
    return pl.pallas_call(
        _front_kernel,
        out_shape=[jax.ShapeDtypeStruct((b, t, n), dt) for n, dt in outs],
        grid=(b, t // tm),
        in_specs=[tile(d), mod, mod, _layer_block(w_all_bf16, layer)] + [const(a) for a in consts],
        out_specs=[tile(n) for n, _ in outs],
        scratch_shapes=[pltpu.VMEM((SUBLANES, tm + CONV_TAIL, dc), F32),
                        pltpu.VMEM((tm, d), BF16),
                        pltpu.VMEM((tm, dr), F32),
                        pltpu.VMEM((1, tm, 4 * dr), F32),
                        pltpu.VMEM((1, tm, dr), BF16),
                        pltpu.VMEM((dr // REC_D, REC_D, REC_D), F32)],
        compiler_params=_cparams(("arbitrary", "arbitrary")),
        name="front",
    )(x, sc, sh, w_all_bf16, *consts)


REC_LEVELS = tuple(CHUNK >> (i + 1) for i in range(int(math.log2(CHUNK))))
REC_SAFE_RANGE = 150.0
REC_SPLIT = 2


def _rec_tables():
    x = np.arange(CHUNK)
    sel = [x[:, None] >= x[None, :]]
    masks = []
    for m in REC_LEVELS:
        bnd = (x // (2 * m)) * (2 * m) + m - 1
        sel.append(bnd[:, None] >= x[None, :])
        same = (x[:, None] // (2 * m)) == (x[None, :] // (2 * m))
        masks.append(same & ((x[:, None] & m) != 0) & ((x[None, :] & m) == 0))
    masks.append(x[:, None] == x[None, :])
    sel = np.concatenate(sel, axis=0).astype(np.float32)
    return np.concatenate([sel] * REC_SPLIT, axis=1), np.stack(masks).astype(np.float32)


def _split_bf16(x):
    parts = []
    for _ in range(REC_SPLIT):
        p = x.astype(BF16)
        parts.append(p)
        x = x - p.astype(F32)
    return parts


def _rec_streams(rf_ref):
    dd = rf_ref.shape[2] // 4
    return [rf_ref.at[:, :, i * dd:(i + 1) * dd] for i in range(4)]


def _rec_body(rf_ref, i_ref, ng_ref, sel_ref, pm_ref, o_ref, st_ref):
    q_ref, lf_ref, k_ref, g_ref = _rec_streams(rf_ref)
    tb = q_ref.shape[1]
    dd = q_ref.shape[2]
    nh = dd // REC_D
    nchunks = tb // CHUNK
    nt = (((1,), (1,)), ((), ()))
    tn = (((0,), (0,)), ((), ()))

    @pl.when(pl.program_id(1) == 0)
    def _():
        st_ref[...] = jnp.zeros(st_ref.shape, F32)

    tot = [jnp.sum(lf_ref[0, c * CHUNK:(c + 1) * CHUNK, :], axis=0, keepdims=True)
           for c in range(nchunks)]
    safe = jnp.min(functools.reduce(jnp.minimum, tot)) >= -REC_SAFE_RANGE

    def prefix(lf, nrows):
        return jnp.dot(sel_ref[0:nrows, :], jnp.concatenate(_split_bf16(lf), axis=0),
                       preferred_element_type=F32)

    def head_out(h, a, qg, kg, vb, eb):
        sl = slice(h * REC_D, (h + 1) * REC_D)
        st = st_ref[h]
        o_h = (lax.dot_general(qg[:, sl], st.astype(BF16), nt, preferred_element_type=F32)
               + jnp.dot(a.astype(BF16), vb[:, sl], preferred_element_type=F32))
        upd = lax.dot_general(vb[:, sl], kg[:, sl], tn, preferred_element_type=F32)
        st_ref[h] = st * eb[:, sl] + upd
        ms = jnp.mean(o_h * o_h, axis=-1, keepdims=True)
        return o_h * lax.rsqrt(ms + LN_EPS) * ng_ref[...]

    def store(rs, outs):
        y = jnp.concatenate(outs, axis=-1) * _silu(g_ref[0, rs, :])
        o_ref[0, rs, :] = y.astype(o_ref.dtype)

    @pl.when(safe)
    def _():
        causal = (lax.broadcasted_iota(jnp.int32, (CHUNK, CHUNK), 0)
                  >= lax.broadcasted_iota(jnp.int32, (CHUNK, CHUNK), 1))
        heads = [slice(h * REC_D, (h + 1) * REC_D) for h in range(nh)]
        rows = [slice(c * CHUNK, (c + 1) * CHUNK) for c in range(nchunks)]
        bcs = [prefix(lf_ref[0, rs, :], CHUNK) for rs in rows]
        qg, kg, vb, eb, att, upd = [], [], [], [], [], []
        for rs, bc in zip(rows, bcs):
            half = 0.5 * bc[CHUNK - 1:CHUNK, :]
            eh = jnp.exp(half)
            qf = q_ref[0, rs, :] * jnp.exp(bc - half)
            kf = k_ref[0, rs, :] * jnp.exp(half - bc)
            qi = qf.astype(BF16)
            ki = kf.astype(BF16)
            qg.append((qf * eh).astype(BF16))
            kg.append((kf * eh).astype(BF16))
            vb.append(i_ref[0, rs, :])
            eb.append(eh * eh)
            att.append([jnp.where(causal, lax.dot_general(qi[:, sl], ki[:, sl], nt,
                                                          preferred_element_type=F32),
                                  0.0).astype(BF16) for sl in heads])
        for c in range(nchunks):
            upd.append([lax.dot_general(vb[c][:, sl], kg[c][:, sl], tn, preferred_element_type=F32)
                        for sl in heads])
        states = []
        for h, sl in enumerate(heads):
            st = st_ref[h]
            per_chunk = []
            for c in range(nchunks):
                per_chunk.append(st.astype(BF16))
                st = st * eb[c][:, sl] + upd[c][h]
            st_ref[h] = st
            states.append(per_chunk)
        for c, rs in enumerate(rows):
            outs = []
            for h, sl in enumerate(heads):
                o_h = (lax.dot_general(qg[c][:, sl], states[h][c], nt, preferred_element_type=F32)
                       + jnp.dot(att[c][h], vb[c][:, sl], preferred_element_type=F32))
                ms = jnp.mean(o_h * o_h, axis=-1, keepdims=True)
                outs.append(o_h * lax.rsqrt(ms + LN_EPS) * ng_ref[...])
            store(rs, outs)

    @pl.when(jnp.logical_not(safe))
    def _():
        row_id = lax.broadcasted_iota(jnp.int32, (CHUNK, dd), 0)

        def chunk_body(ci, carry):
            rs = pl.ds(pl.multiple_of(ci * CHUNK, CHUNK), CHUNK)
            q = q_ref[0, rs, :]
            k = k_ref[0, rs, :]
            pre = prefix(lf_ref[0, rs, :], sel_ref.shape[0])
            bc = pre[0:CHUNK, :]
            bl = bc[CHUNK - 1:CHUNK, :]
            qg = (q * jnp.exp(bc)).astype(BF16)
            kg = (k * jnp.exp(bl - bc)).astype(BF16)
            vb = i_ref[0, rs, :]
            xs = []
            for li, m in enumerate(REC_LEVELS):
                dec = jnp.exp(-jnp.abs(bc - pre[(li + 1) * CHUNK:(li + 2) * CHUNK, :]))
                xs.append((jnp.where((row_id & m) != 0, q, k) * dec).astype(BF16))
            qb16 = q.astype(BF16)
            kb16 = k.astype(BF16)
            outs = []
            for h in range(nh):
                sl = slice(h * REC_D, (h + 1) * REC_D)
                a = (lax.dot_general(qb16[:, sl], kb16[:, sl], nt, preferred_element_type=F32)
                     * pm_ref[len(REC_LEVELS)])
                for li in range(len(REC_LEVELS)):
                    xh = xs[li][:, sl]
                    a = a + lax.dot_general(xh, xh, nt, preferred_element_type=F32) * pm_ref[li]
                outs.append(head_out(h, a, qg, kg, vb, jnp.exp(bl)))
            store(rs, outs)
            return carry

        lax.fori_loop(0, nchunks, chunk_body, 0)


def _att_kernel(q_ref, *refs):
    nblk = (len(refs) - 3) // 2
    k_refs, v_refs = refs[:nblk], refs[nblk:2 * nblk]
    g_ref, o_ref, bias_ref = refs[2 * nblk:]
    qb = q_ref.shape[1]
    da = q_ref.shape[2]
    hist = (nblk - 1) * qb
    t = pl.program_id(1)

    @pl.when((pl.program_id(0) == 0) & (t == 0))
    def _():
        gw = g_ref.shape[2]
        qrow = lax.broadcasted_iota(jnp.int32, (qb, hist + qb), 0)
        kcol = lax.broadcasted_iota(jnp.int32, (qb, hist + qb), 1)
        dchunk = kcol // CHUNK - qrow // CHUNK
        band = (dchunk >= 0) & (dchunk <= ATT_LEFT)
        for h in range(da // ATT_DH):
            gb = jnp.broadcast_to(g_ref[h], (qb, gw))
            skew = pltpu.roll(gb, gw - (qb - 1), 1, stride=1, stride_axis=0)
            bias_ref[h] = jnp.where(band, skew[:, :hist + qb] * LOG2E, NEG_BIG)

    nh = da // ATT_DH
    lane_head = lax.broadcasted_iota(jnp.int32, (qb, da), 1) // ATT_DH

    def attend(mask_start):
        q = q_ref[0]
        kk = jnp.concatenate([r[0] for r in k_refs], axis=0)
        vv = jnp.concatenate([r[0] for r in v_refs], axis=0)
        scores = [lax.dot_general(jnp.where(lane_head == h, q, jnp.zeros_like(q)), kk,
                                  (((1,), (1,)), ((), ())), preferred_element_type=F32)
                  for h in range(nh)]
        if mask_start:
            kidx = lax.broadcasted_iota(jnp.int32, (qb, hist + qb), 1)
            in_seq = kidx >= hist - t * qb
        out = jnp.zeros((qb, da), F32)
        for h in range(nh):
            s = scores[h] + bias_ref[h]
            if mask_start:
                s = jnp.where(in_seq, s, NEG_BIG)
            e = jnp.exp2(s - jnp.max(s, axis=-1, keepdims=True))
            l = jnp.sum(e, axis=-1, keepdims=True)
            oh = jnp.dot(e.astype(BF16), vv, preferred_element_type=F32)
            out = out + jnp.where(lane_head == h, oh / l, 0.0)
        o_ref[0] = out.astype(o_ref.dtype)

    @pl.when(t * qb < hist)
    def _():
        attend(True)

    @pl.when(t * qb >= hist)
    def _():
        attend(False)


def _att_rel_row(rel_table, qb):
    hist = ATT_LEFT * CHUNK
    gw = pl.next_power_of_2(2 * qb + hist - 1)
    n_lo = qb - 1 + hist - MAX_REL
    n_hi = gw - n_lo - (2 * MAX_REL + 1)
    nh = rel_table.shape[0]
    tab = rel_table.astype(F32)
    row = jnp.concatenate([jnp.broadcast_to(tab[:, :1], (nh, n_lo)), tab,
                           jnp.broadcast_to(tab[:, -1:], (nh, n_hi))], axis=1)
    return row.reshape(nh, 1, gw)


def _att_mixer(q, k, v, rel_table, qb):
    b, t, da = q.shape
    hist = ATT_LEFT * CHUNK
    grow = _att_rel_row(rel_table, qb)
    tile = pl.BlockSpec((1, qb, da), lambda i, j: (i, j, 0))
    back = lambda n: pl.BlockSpec((1, qb, da), lambda i, j: (i, jnp.maximum(j - n, 0), 0))
    blocks = [back(n) for n in range(hist // qb, 0, -1)] + [tile]
    return pl.pallas_call(
        _att_kernel,
        out_shape=jax.ShapeDtypeStruct((b, t, da), BF16),
        grid=(b, t // qb),
        in_specs=[tile] + blocks + blocks + [pl.BlockSpec(grow.shape, lambda i, j: (0, 0, 0))],
        out_specs=tile,
        scratch_shapes=[pltpu.VMEM((ATT_HEADS, qb, hist + qb), F32)],
        compiler_params=_cparams(("arbitrary", "arbitrary")),
        name="att_mixer",
    )(q, *([k] * len(blocks)), *([v] * len(blocks)), grow)


def _tail_kernel(x_ref, yc_ref, yr_ref, ya_ref, wo_ref, g1_ref, l1g_ref, l1b_ref,
                 sc_ref, sh_ref, g2_ref, w1_ref, w2_ref, l2g_ref, l2b_ref, o_ref, a_s, x1_s, h_s,
                 *, alpha, d_ff, fc):
    tm = x_ref.shape[1]
    groups = [slice(i * tm // TAIL_SPLIT, (i + 1) * tm // TAIL_SPLIT) for i in range(TAIL_SPLIT)]
    ys = [jnp.dot(jnp.concatenate([yc_ref[0, rs, :], yr_ref[0, rs, :], ya_ref[0, rs, :]], axis=-1),
                  wo_ref[...], preferred_element_type=F32) for rs in groups]
    for rs, y in zip(groups, ys):
        x1 = _layer_norm(alpha * x_ref[0, rs, :] + (1.0 + g1_ref[0]) * y, l1g_ref[...], l1b_ref[...])
        x1_s[rs, :] = x1
        h_s[rs, :] = (x1 * (1.0 + sc_ref[0]) + sh_ref[0]).astype(BF16)
    for c in range(d_ff // fc):
        for rs in groups:
            gt = jnp.dot(h_s[rs, :], w1_ref[:, c * fc:(c + 1) * fc], preferred_element_type=F32)
            up = jnp.dot(h_s[rs, :], w1_ref[:, d_ff + c * fc:d_ff + (c + 1) * fc],
                         preferred_element_type=F32)
            a_s[rs, c * fc:(c + 1) * fc] = (_silu(gt) * up).astype(BF16)
    y2s = [jnp.dot(a_s[rs, :], w2_ref[...], preferred_element_type=F32) for rs in groups]
    for rs, y2 in zip(groups, y2s):
        o_ref[0, rs, :] = _layer_norm(alpha * x1_s[rs, :] + (1.0 + g2_ref[0]) * y2,
                                      l2g_ref[...], l2b_ref[...])


def _tail(x, yc, yr, ya, wo_bf16, g1, ln1_g, ln1_b, sc2, sh2, g2, w1_bf16, w2_bf16, ln2_g, ln2_b,
          layer, alpha, tm):
    b, t, d = x.shape
    d_ff = w2_bf16.shape[1]
    tile = lambda a: pl.BlockSpec((1, tm, a.shape[2]), lambda i, j: (i, j, 0))
    mod = pl.BlockSpec((1, 1, d), lambda i, j: (i, 0, 0))
    row = pl.BlockSpec((1, d), lambda i, j: (0, 0))
    weight = lambda a: _layer_block(a, layer, pipeline_mode=pl.Buffered(1))
    return pl.pallas_call(
        functools.partial(_tail_kernel, alpha=alpha, d_ff=d_ff, fc=FFN_FC),
        out_shape=jax.ShapeDtypeStruct((b, t, d), F32),
        grid=(b, t // tm),
        in_specs=[tile(x), tile(yc), tile(yr), tile(ya), weight(wo_bf16), mod, row, row,
                  mod, mod, mod, weight(w1_bf16), weight(w2_bf16), row, row],
        out_specs=tile(x),
        scratch_shapes=[pltpu.VMEM((tm, d_ff), BF16),
                        pltpu.VMEM((tm, d), F32),
                        pltpu.VMEM((tm, d), BF16)],
        compiler_params=_cparams(("arbitrary", "arbitrary")),
        name="tail",
    )(x, yc, yr, ya, wo_bf16, g1, ln1_g.reshape(1, d), ln1_b.reshape(1, d), sc2, sh2, g2,
      w1_bf16, w2_bf16, ln2_g.reshape(1, d), ln2_b.reshape(1, d))


def kernel(x, c, w_ada, b_ada, w_in, conv_w, conv_b, conv_ln_g, conv_ln_b, rec_lower_bound,
           rec_norm_g, rel_bias, w_out, ln1_g, ln1_b, ln2_g, ln2_b, w_ffn_in, w_ffn_out):
    depth = w_in.shape[0]
    b, t, d = x.shape
    alpha = (2 * depth) ** 0.25

    lbs = jax.nn.softmax(rec_lower_bound.astype(F32), axis=0)
    lbs = jnp.cumsum(lbs, axis=0) - lbs[0]
    mods = _ada_mod(c, w_ada, b_ada)
    w_in, w_out, w_ffn_in, w_ffn_out = [w.astype(BF16) for w in (w_in, w_out, w_ffn_in, w_ffn_out)]

    for l in range(depth):
        sh1, sc1, g1, sh2, sc2, g2 = [m.reshape(b, 1, d) for m in jnp.split(mods[l], 6, axis=-1)]
        y_conv, y_rec, aq, ak, av = _front(
            x, sc1, sh1, w_in, l, conv_w[l], conv_b[l], conv_ln_g[l], conv_ln_b[l], lbs[l],
            rec_norm_g[l], tm=FRONT_ROWS)
        y_att = _att_mixer(aq, ak, av, rel_bias[l], qb=ATT_QB)
        x = _tail(x, y_conv, y_rec, y_att, w_out, g1, ln1_g[l], ln1_b[l], sc2, sh2, g2,
                  w_ffn_in, w_ffn_out, ln2_g[l], ln2_b[l], l, alpha, tm=TILE_ROWS)
    return x
```

```python
import functools
import math

import numpy as np
import jax
import jax.numpy as jnp
from jax import lax
from jax.experimental import pallas as pl
from jax.experimental.pallas import tpu as pltpu

F32 = jnp.float32
BF16 = jnp.bfloat16

CHUNK = 64
CONV_WIDTH = 31
REC_HEADS = 4
REC_D = 128
ATT_HEADS = 4
ATT_DH = 64
ATT_LEFT = 8
MAX_REL = 128
LN_EPS = 1e-5
NEG_BIG = -1e30
TINY = 1e-30
LOG2E = math.log2(math.e)

ADA_COLS = 3072
FRONT_ROWS = 512
TILE_ROWS = 1024
ATT_QB = 256
CONV_TAIL = 32
CONV_RS = 16
GATE_RS = 16
FFN_FC = 256
TAIL_SPLIT = 4
SUBLANES = 8

VMEM_LIMIT = 56 * 1024 * 1024


def _cparams(sem):
    return pltpu.CompilerParams(dimension_semantics=sem, vmem_limit_bytes=VMEM_LIMIT)


def _layer_norm(x, g, b):
    mu = jnp.mean(x, axis=-1, keepdims=True)
    xc = x - mu
    var = jnp.mean(xc * xc, axis=-1, keepdims=True)
    return xc * lax.rsqrt(var + LN_EPS) * g + b


def _silu(x):
    return x * jax.nn.sigmoid(x)


def _ada_kernel(c_ref, w_ref, b_ref, o_ref):
    ca = _silu(c_ref[...]).astype(BF16)
    o_ref[0] = jnp.dot(ca, w_ref[0].astype(BF16), preferred_element_type=F32) + b_ref[0]


def _ada_mod(c, w_ada, b_ada):
    depth, d, n6 = w_ada.shape
    b = c.shape[0]
    bp = -(-b // SUBLANES) * SUBLANES
    cp = jnp.zeros((bp, d), F32).at[:b].set(c)
    tn = ADA_COLS
    out = pl.pallas_call(
        _ada_kernel,
        out_shape=jax.ShapeDtypeStruct((depth, bp, n6), F32),
        grid=(depth, n6 // tn),
        in_specs=[pl.BlockSpec((bp, d), lambda l, j: (0, 0)),
                  pl.BlockSpec((1, d, tn), lambda l, j: (l, 0, j)),
                  pl.BlockSpec((1, 1, tn), lambda l, j: (l, 0, j))],
        out_specs=pl.BlockSpec((1, bp, tn), lambda l, j: (l, 0, j)),
        compiler_params=_cparams(("arbitrary", "arbitrary")),
        name="ada_mod",
    )(cp, w_ada, b_ada.reshape(depth, 1, n6))
    return out[:, :b]


def _front_kernel(x_ref, sc_ref, sh_ref, w_ref, cw_ref, cb_ref, cg_ref, cbe_ref, lb_ref,
                  ng_ref, sel_ref, pm_ref,
                  yc_ref, yr_ref, aq_ref, ak_ref, av_ref,
                  shs, h_s, z_s, rf_ref, ri_ref, st_ref):
    rq_ref, lf_ref, rk_ref, rg_ref = _rec_streams(rf_ref)
    tm = x_ref.shape[1]
    dc = cw_ref.shape[1]
    dr = lb_ref.shape[1]
    da = aq_ref.shape[2]
    rec0 = 2 * dc
    att0 = 2 * dc + 4 * dr
    quarter = tm // 4

    def stage(fn):
        fn()

    def proj(c0, n):
        return jnp.dot(h_s[...], w_ref[:, c0:c0 + n], preferred_element_type=F32)

    def paced_zero(dep):
        bits = lax.bitcast_convert_type(dep[0:SUBLANES, 0:128], jnp.uint32)
        zero = lax.shift_right_logical(lax.shift_right_logical(bits, jnp.uint32(16)), jnp.uint32(16))
        zero = lax.bitcast_convert_type(zero, F32)
        return jnp.tile(zero, (CONV_RS // SUBLANES, dc // 128))

    def conv_rows(lo, hi, dep):
        off = CONV_TAIL - (CONV_WIDTH - 1)
        acc = dep
        for r0 in range(lo, hi, CONV_RS):
            acc = paced_zero(acc)
            for j in range(CONV_WIDTH):
                a, s = divmod(off + j, SUBLANES)
                acc = acc + cw_ref[j:j + 1, :] * shs[s, r0 + a * SUBLANES:r0 + a * SUBLANES + CONV_RS, :]
            y = _layer_norm(acc + cb_ref[...], cg_ref[...], cbe_ref[...])
            yc_ref[0, r0:r0 + CONV_RS, :] = _silu(y).astype(BF16)

    def gate_rows(lo, hi):
        lb = lb_ref[...]
        lb_floor = jnp.maximum(lb, TINY)
        one_m_lb = 1.0 - lb
        for r0 in range(lo, hi, GATE_RS):
            z = z_s[r0:r0 + GATE_RS, :]
            e = jnp.exp(-jnp.abs(z))
            r = 1.0 / (1.0 + e)
            pos = z >= 0.0
            keep = one_m_lb * (jnp.where(pos, 1.0, e) * r)
            lf_ref[0, r0:r0 + GATE_RS, :] = jnp.log(lb_floor + keep)
            rk_ref[0, r0:r0 + GATE_RS, :] = one_m_lb - keep

    @pl.when(pl.program_id(1) == 0)
    def _():
        shs[0, 0:CONV_TAIL, :] = jnp.zeros((CONV_TAIL, dc), F32)

    @pl.when(pl.program_id(1) > 0)
    def _():
        shs[0, 0:CONV_TAIL, :] = shs[0, tm:tm + CONV_TAIL, :]

    h_s[...] = (x_ref[0] * (1.0 + sc_ref[0]) + sh_ref[0]).astype(BF16)

    @stage
    def _():
        pc = proj(0, 2 * dc)
        for r0 in range(0, tm, CONV_RS):
            blk = pc[r0:r0 + CONV_RS, :]
            shs[0, CONV_TAIL + r0:CONV_TAIL + r0 + CONV_RS, :] = (
                blk[:, :dc] * jax.nn.sigmoid(blk[:, dc:]))

    @stage
    def _():
        rq_ref[0] = proj(rec0, dr)
        n_sh = tm + CONV_TAIL - SUBLANES
        for s in range(1, SUBLANES):
            for r0 in range(0, n_sh, CONV_RS):
                n = min(CONV_RS, n_sh - r0)
                shs[s, r0:r0 + n, :] = shs[0, r0 + s:r0 + s + n, :]

    @stage
    def _():
        res = proj(rec0 + dr, dr)
        z_s[...] = res
        conv_rows(0, quarter, res)

    @stage
    def _():
        res = proj(rec0 + 2 * dr, dr)
        ri_ref[0] = res.astype(BF16)
        conv_rows(quarter, 2 * quarter, res)
        gate_rows(0, 2 * quarter)

    @stage
    def _():
        res = proj(rec0 + 3 * dr, dr)
        rg_ref[0] = res
        conv_rows(2 * quarter, 3 * quarter, res)
        gate_rows(2 * quarter, tm)

    @stage
    def _():
        aq_ref[0] = (proj(att0, da) * (LOG2E * ATT_DH ** -0.5)).astype(BF16)
        ak_ref[0] = proj(att0 + da, da).astype(BF16)
        res = proj(att0 + 2 * da, da)
        av_ref[0] = res.astype(BF16)
        conv_rows(3 * quarter, tm, res)

    _rec_body(rf_ref, ri_ref, ng_ref, sel_ref, pm_ref, yr_ref, st_ref)


def _layer_block(stacked, layer, **kw):
    return pl.BlockSpec((None,) + stacked.shape[1:], lambda i, j: (layer, 0, 0), **kw)


def _front(x, sc, sh, w_all_bf16, layer, conv_w, conv_b, conv_ln_g, conv_ln_b, lb, norm_g, tm):
    b, t, d = x.shape
    dc = conv_w.shape[1]
    dr = lb.shape[0]
    da = ATT_HEADS * ATT_DH
    row = lambda a: a.reshape(1, -1)
    const = lambda a: pl.BlockSpec(a.shape, lambda i, j: (0,) * a.ndim)
    mod = pl.BlockSpec((1, 1, d), lambda i, j: (i, 0, 0))
    tile = lambda n: pl.BlockSpec((1, tm, n), lambda i, j: (i, j, 0))
    outs = [(dc, BF16), (dr, BF16), (da, BF16), (da, BF16), (da, BF16)]
    sel, pmask = _rec_tables()
    consts = [conv_w, row(conv_b), row(conv_ln_g), row(conv_ln_b), row(lb), row(norm_g),
              jnp.asarray(sel, BF16), jnp.asarray(pmask, F32)]
    return pl.pallas_call(
        _front_kernel,
        out_shape=[jax.ShapeDtypeStruct((b, t, n), dt) for n, dt in outs],
        grid=(b, t // tm),
        in_specs=[tile(d), mod, mod, _layer_block(w_all_bf16, layer)] + [const(a) for a in consts],
        out_specs=[tile(n) for n, _ in outs],
        scratch_shapes=[pltpu.VMEM((SUBLANES, tm + CONV_TAIL, dc), F32),
                        pltpu.VMEM((tm, d), BF16),
                        pltpu.VMEM((tm, dr), F32),
                        pltpu.VMEM((1, tm, 4 * dr), F32),
                        pltpu.VMEM((1, tm, dr), BF16),
                        pltpu.VMEM((dr // REC_D, REC_D, REC_D), F32)],
        compiler_params=_cparams(("arbitrary", "arbitrary")),
        name="front",
    )(x, sc, sh, w_all_bf16, *consts)


REC_LEVELS = tuple(CHUNK >> (i + 1) for i in range(int(math.log2(CHUNK))))
REC_SAFE_RANGE = 150.0
REC_SPLIT = 2


def _rec_tables():
    x = np.arange(CHUNK)
    sel = [x[:, None] >= x[None, :]]
    masks = []
    for m in REC_LEVELS:
        bnd = (x // (2 * m)) * (2 * m) + m - 1
        sel.append(bnd[:, None] >= x[None, :])
        same = (x[:, None] // (2 * m)) == (x[None, :] // (2 * m))
        masks.append(same & ((x[:, None] & m) != 0) & ((x[None, :] & m) == 0))
    masks.append(x[:, None] == x[None, :])
    sel = np.concatenate(sel, axis=0).astype(np.float32)
    return np.concatenate([sel] * REC_SPLIT, axis=1), np.stack(masks).astype(np.float32)


def _split_bf16(x):
    parts = []
    for _ in range(REC_SPLIT):
        p = x.astype(BF16)
        parts.append(p)
        x = x - p.astype(F32)
    return parts


def _rec_streams(rf_ref):
    dd = rf_ref.shape[2] // 4
    return [rf_ref.at[:, :, i * dd:(i + 1) * dd] for i in range(4)]


def _rec_body(rf_ref, i_ref, ng_ref, sel_ref, pm_ref, o_ref, st_ref):
    q_ref, lf_ref, k_ref, g_ref = _rec_streams(rf_ref)
    tb = q_ref.shape[1]
    dd = q_ref.shape[2]
    nh = dd // REC_D
    nchunks = tb // CHUNK
    nt = (((1,), (1,)), ((), ()))
    tn = (((0,), (0,)), ((), ()))

    @pl.when(pl.program_id(1) == 0)
    def _():
        st_ref[...] = jnp.zeros(st_ref.shape, F32)

    tot = [jnp.sum(lf_ref[0, c * CHUNK:(c + 1) * CHUNK, :], axis=0, keepdims=True)
           for c in range(nchunks)]
    safe = jnp.min(functools.reduce(jnp.minimum, tot)) >= -REC_SAFE_RANGE

    def prefix(lf, nrows):
        return jnp.dot(sel_ref[0:nrows, :], jnp.concatenate(_split_bf16(lf), axis=0),
                       preferred_element_type=F32)

    def head_out(h, a, qg, kg, vb, eb):
        sl = slice(h * REC_D, (h + 1) * REC_D)
        st = st_ref[h]
        o_h = (lax.dot_general(qg[:, sl], st.astype(BF16), nt, preferred_element_type=F32)
               + jnp.dot(a.astype(BF16), vb[:, sl], preferred_element_type=F32))
        upd = lax.dot_general(vb[:, sl], kg[:, sl], tn, preferred_element_type=F32)
        st_ref[h] = st * eb[:, sl] + upd
        ms = jnp.mean(o_h * o_h, axis=-1, keepdims=True)
        return o_h * lax.rsqrt(ms + LN_EPS) * ng_ref[...]

    def store(rs, outs):
        y = jnp.concatenate(outs, axis=-1) * _silu(g_ref[0, rs, :])
        o_ref[0, rs, :] = y.astype(o_ref.dtype)

    @pl.when(safe)
    def _():
        causal = (lax.broadcasted_iota(jnp.int32, (CHUNK, CHUNK), 0)
                  >= lax.broadcasted_iota(jnp.int32, (CHUNK, CHUNK), 1))
        heads = [slice(h * REC_D, (h + 1) * REC_D) for h in range(nh)]
        rows = [slice(c * CHUNK, (c + 1) * CHUNK) for c in range(nchunks)]
        bcs = [prefix(lf_ref[0, rs, :], CHUNK) for rs in rows]
        qg, kg, vb, eb, att, upd = [], [], [], [], [], []
        for rs, bc in zip(rows, bcs):
            half = 0.5 * bc[CHUNK - 1:CHUNK, :]
            eh = jnp.exp(half)
            qf = q_ref[0, rs, :] * jnp.exp(bc - half)
            kf = k_ref[0, rs, :] * jnp.exp(half - bc)
            qi = qf.astype(BF16)
            ki = kf.astype(BF16)
            qg.append((qf * eh).astype(BF16))
            kg.append((kf * eh).astype(BF16))
            vb.append(i_ref[0, rs, :])
            eb.append(eh * eh)
            att.append([jnp.where(causal, lax.dot_general(qi[:, sl], ki[:, sl], nt,
                                                          preferred_element_type=F32),
                                  0.0).astype(BF16) for sl in heads])
        for c in range(nchunks):
            upd.append([lax.dot_general(vb[c][:, sl], kg[c][:, sl], tn, preferred_element_type=F32)
                        for sl in heads])
        states = []
        for h, sl in enumerate(heads):
            st = st_ref[h]
            per_chunk = []
            for c in range(nchunks):
                per_chunk.append(st.astype(BF16))
                st = st * eb[c][:, sl] + upd[c][h]
            st_ref[h] = st
            states.append(per_chunk)
        for c, rs in enumerate(rows):
            outs = []
            for h, sl in enumerate(heads):
                o_h = (lax.dot_general(qg[c][:, sl], states[h][c], nt, preferred_element_type=F32)
                       + jnp.dot(att[c][h], vb[c][:, sl], preferred_element_type=F32))
                ms = jnp.mean(o_h * o_h, axis=-1, keepdims=True)
                outs.append(o_h * lax.rsqrt(ms + LN_EPS) * ng_ref[...])
            store(rs, outs)

    @pl.when(jnp.logical_not(safe))
    def _():
        row_id = lax.broadcasted_iota(jnp.int32, (CHUNK, dd), 0)

        def chunk_body(ci, carry):
            rs = pl.ds(pl.multiple_of(ci * CHUNK, CHUNK), CHUNK)
            q = q_ref[0, rs, :]
            k = k_ref[0, rs, :]
            pre = prefix(lf_ref[0, rs, :], sel_ref.shape[0])
            bc = pre[0:CHUNK, :]
            bl = bc[CHUNK - 1:CHUNK, :]
            qg = (q * jnp.exp(bc)).astype(BF16)
            kg = (k * jnp.exp(bl - bc)).astype(BF16)
            vb = i_ref[0, rs, :]
            xs = []
            for li, m in enumerate(REC_LEVELS):
                dec = jnp.exp(-jnp.abs(bc - pre[(li + 1) * CHUNK:(li + 2) * CHUNK, :]))
                xs.append((jnp.where((row_id & m) != 0, q, k) * dec).astype(BF16))
            qb16 = q.astype(BF16)
            kb16 = k.astype(BF16)
            outs = []
            for h in range(nh):
                sl = slice(h * REC_D, (h + 1) * REC_D)
                a = (lax.dot_general(qb16[:, sl], kb16[:, sl], nt, preferred_element_type=F32)
                     * pm_ref[len(REC_LEVELS)])
                for li in range(len(REC_LEVELS)):
                    xh = xs[li][:, sl]
                    a = a + lax.dot_general(xh, xh, nt, preferred_element_type=F32) * pm_ref[li]
                outs.append(head_out(h, a, qg, kg, vb, jnp.exp(bl)))
            store(rs, outs)
            return carry

        lax.fori_loop(0, nchunks, chunk_body, 0)


def _att_kernel(q_ref, *refs):
    nblk = (len(refs) - 3) // 2
    k_refs, v_refs = refs[:nblk], refs[nblk:2 * nblk]
    g_ref, o_ref, bias_ref = refs[2 * nblk:]
    qb = q_ref.shape[1]
    da = q_ref.shape[2]
    hist = (nblk - 1) * qb
    t = pl.program_id(1)

    @pl.when((pl.program_id(0) == 0) & (t == 0))
    def _():
        gw = g_ref.shape[2]
        qrow = lax.broadcasted_iota(jnp.int32, (qb, hist + qb), 0)
        kcol = lax.broadcasted_iota(jnp.int32, (qb, hist + qb), 1)
        dchunk = kcol // CHUNK - qrow // CHUNK
        band = (dchunk >= 0) & (dchunk <= ATT_LEFT)
        for h in range(da // ATT_DH):
            gb = jnp.broadcast_to(g_ref[h], (qb, gw))
            skew = pltpu.roll(gb, gw - (qb - 1), 1, stride=1, stride_axis=0)
            bias_ref[h] = jnp.where(band, skew[:, :hist + qb] * LOG2E, NEG_BIG)

    nh = da // ATT_DH
    lane_head = lax.broadcasted_iota(jnp.int32, (qb, da), 1) // ATT_DH

    def attend(mask_start):
        q = q_ref[0]
        kk = jnp.concatenate([r[0] for r in k_refs], axis=0)
        vv = jnp.concatenate([r[0] for r in v_refs], axis=0)
        scores = [lax.dot_general(jnp.where(lane_head == h, q, jnp.zeros_like(q)), kk,
                                  (((1,), (1,)), ((), ())), preferred_element_type=F32)
                  for h in range(nh)]
        if mask_start:
            kidx = lax.broadcasted_iota(jnp.int32, (qb, hist + qb), 1)
            in_seq = kidx >= hist - t * qb
        out = jnp.zeros((qb, da), F32)
        for h in range(nh):
            s = scores[h] + bias_ref[h]
            if mask_start:
                s = jnp.where(in_seq, s, NEG_BIG)
            e = jnp.exp2(s - jnp.max(s, axis=-1, keepdims=True))
            l = jnp.sum(e, axis=-1, keepdims=True)
            oh = jnp.dot(e.astype(BF16), vv, preferred_element_type=F32)
            out = out + jnp.where(lane_head == h, oh / l, 0.0)
        o_ref[0] = out.astype(o_ref.dtype)

    @pl.when(t * qb < hist)
    def _():
        attend(True)

    @pl.when(t * qb >= hist)
    def _():
        attend(False)


def _att_rel_row(rel_table, qb):
    hist = ATT_LEFT * CHUNK
    gw = pl.next_power_of_2(2 * qb + hist - 1)
    n_lo = qb - 1 + hist - MAX_REL
    n_hi = gw - n_lo - (2 * MAX_REL + 1)
    nh = rel_table.shape[0]
    tab = rel_table.astype(F32)
    row = jnp.concatenate([jnp.broadcast_to(tab[:, :1], (nh, n_lo)), tab,
                           jnp.broadcast_to(tab[:, -1:], (nh, n_hi))], axis=1)
    return row.reshape(nh, 1, gw)


def _att_mixer(q, k, v, rel_table, qb):
    b, t, da = q.shape
    hist = ATT_LEFT * CHUNK
    grow = _att_rel_row(rel_table, qb)
    tile = pl.BlockSpec((1, qb, da), lambda i, j: (i, j, 0))
    back = lambda n: pl.BlockSpec((1, qb, da), lambda i, j: (i, jnp.maximum(j - n, 0), 0))
    blocks = [back(n) for n in range(hist // qb, 0, -1)] + [tile]
    return pl.pallas_call(
        _att_kernel,
        out_shape=jax.ShapeDtypeStruct((b, t, da), BF16),
        grid=(b, t // qb),
        in_specs=[tile] + blocks + blocks + [pl.BlockSpec(grow.shape, lambda i, j: (0, 0, 0))],
        out_specs=tile,
        scratch_shapes=[pltpu.VMEM((ATT_HEADS, qb, hist + qb), F32)],
        compiler_params=_cparams(("arbitrary", "arbitrary")),
        name="att_mixer",
    )(q, *([k] * len(blocks)), *([v] * len(blocks)), grow)


def _tail_kernel(x_ref, yc_ref, yr_ref, ya_ref, wo_ref, g1_ref, l1g_ref, l1b_ref,
                 sc_ref, sh_ref, g2_ref, w1_ref, w2_ref, l2g_ref, l2b_ref, o_ref, a_s, x1_s, h_s,
                 *, alpha, d_ff, fc):
    tm = x_ref.shape[1]
    groups = [slice(i * tm // TAIL_SPLIT, (i + 1) * tm // TAIL_SPLIT) for i in range(TAIL_SPLIT)]
    ys = [jnp.dot(jnp.concatenate([yc_ref[0, rs, :], yr_ref[0, rs, :], ya_ref[0, rs, :]], axis=-1),
                  wo_ref[...], preferred_element_type=F32) for rs in groups]
    for rs, y in zip(groups, ys):
        x1 = _layer_norm(alpha * x_ref[0, rs, :] + (1.0 + g1_ref[0]) * y, l1g_ref[...], l1b_ref[...])
        x1_s[rs, :] = x1
        h_s[rs, :] = (x1 * (1.0 + sc_ref[0]) + sh_ref[0]).astype(BF16)
    for c in range(d_ff // fc):
        for rs in groups:
            gt = jnp.dot(h_s[rs, :], w1_ref[:, c * fc:(c + 1) * fc], preferred_element_type=F32)
            up = jnp.dot(h_s[rs, :], w1_ref[:, d_ff + c * fc:d_ff + (c + 1) * fc],
                         preferred_element_type=F32)
            a_s[rs, c * fc:(c + 1) * fc] = (_silu(gt) * up).astype(BF16)
    y2s = [jnp.dot(a_s[rs, :], w2_ref[...], preferred_element_type=F32) for rs in groups]
    for rs, y2 in zip(groups, y2s):
        o_ref[0, rs, :] = _layer_norm(alpha * x1_s[rs, :] + (1.0 + g2_ref[0]) * y2,
                                      l2g_ref[...], l2b_ref[...])


def _tail(x, yc, yr, ya, wo_bf16, g1, ln1_g, ln1_b, sc2, sh2, g2, w1_bf16, w2_bf16, ln2_g, ln2_b,
          layer, alpha, tm):
    b, t, d = x.shape
    d_ff = w2_bf16.shape[1]
    tile = lambda a: pl.BlockSpec((1, tm, a.shape[2]), lambda i, j: (i, j, 0))
    mod = pl.BlockSpec((1, 1, d), lambda i, j: (i, 0, 0))
    row = pl.BlockSpec((1, d), lambda i, j: (0, 0))
    weight = lambda a: _layer_block(a, layer, pipeline_mode=pl.Buffered(1))
    return pl.pallas_call(
        functools.partial(_tail_kernel, alpha=alpha, d_ff=d_ff, fc=FFN_FC),
        out_shape=jax.ShapeDtypeStruct((b, t, d), F32),
        grid=(b, t // tm),
        in_specs=[tile(x), tile(yc), tile(yr), tile(ya), weight(wo_bf16), mod, row, row,
                  mod, mod, mod, weight(w1_bf16), weight(w2_bf16), row, row],
        out_specs=tile(x),
        scratch_shapes=[pltpu.VMEM((tm, d_ff), BF16),
                        pltpu.VMEM((tm, d), F32),
                        pltpu.VMEM((tm, d), BF16)],
        compiler_params=_cparams(("arbitrary", "arbitrary")),
        name="tail",
    )(x, yc, yr, ya, wo_bf16, g1, ln1_g.reshape(1, d), ln1_b.reshape(1, d), sc2, sh2, g2,
      w1_bf16, w2_bf16, ln2_g.reshape(1, d), ln2_b.reshape(1, d))


def kernel(x, c, w_ada, b_ada, w_in, conv_w, conv_b, conv_ln_g, conv_ln_b, rec_lower_bound,
           rec_norm_g, rel_bias, w_out, ln1_g, ln1_b, ln2_g, ln2_b, w_ffn_in, w_ffn_out):
    depth = w_in.shape[0]
    b, t, d = x.shape
    alpha = (2 * depth) ** 0.25

    lbs = jax.nn.softmax(rec_lower_bound.astype(F32), axis=0)
    lbs = jnp.cumsum(lbs, axis=0) - lbs[0]
    mods = _ada_mod(c, w_ada, b_ada)
    w_in, w_out, w_ffn_in, w_ffn_out = [w.astype(BF16) for w in (w_in, w_out, w_ffn_in, w_ffn_out)]

    for l in range(depth):
        sh1, sc1, g1, sh2, sc2, g2 = [m.reshape(b, 1, d) for m in jnp.split(mods[l], 6, axis=-1)]
        y_conv, y_rec, aq, ak, av = _front(
            x, sc1, sh1, w_in, l, conv_w[l], conv_b[l], conv_ln_g[l], conv_ln_b[l], lbs[l],
            rec_norm_g[l], tm=FRONT_ROWS)
        y_att = _att_mixer(aq, ak, av, rel_bias[l], qb=ATT_QB)
        x = _tail(x, y_conv, y_rec, y_att, w_out, g1, ln1_g[l], ln1_b[l], sc2, sh2, g2,
                  w_ffn_in, w_ffn_out, ln2_g[l], ln2_b[l], l, alpha, tm=TILE_ROWS)
    return x
```

```python
import functools
import math

import numpy as np
import jax
import jax.numpy as jnp
from jax import lax
from jax.experimental import pallas as pl
from jax.experimental.pallas import tpu as pltpu

F32 = jnp.float32
BF16 = jnp.bfloat16

CHUNK = 64
CONV_WIDTH = 31
REC_HEADS = 4
REC_D = 128
ATT_HEADS = 4
ATT_DH = 64
ATT_LEFT = 8
MAX_REL = 128
LN_EPS = 1e-5
NEG_BIG = -1e30
TINY = 1e-30
LOG2E = math.log2(math.e)

ADA_COLS = 3072
FRONT_ROWS = 512
TILE_ROWS = 1024
ATT_QB = 256
CONV_TAIL = 32
CONV_RS = 32
GATE_RS = 16
FFN_FC = 256
TAIL_SPLIT = 4
SUBLANES = 8

VMEM_LIMIT = 56 * 1024 * 1024


def _cparams(sem):
    return pltpu.CompilerParams(dimension_semantics=sem, vmem_limit_bytes=VMEM_LIMIT)


def _layer_norm(x, g, b):
    mu = jnp.mean(x, axis=-1, keepdims=True)
    xc = x - mu
    var = jnp.mean(xc * xc, axis=-1, keepdims=True)
    return xc * lax.rsqrt(var + LN_EPS) * g + b


def _silu(x):
    return x * jax.nn.sigmoid(x)


def _ada_kernel(c_ref, w_ref, b_ref, o_ref):
    ca = _silu(c_ref[...]).astype(BF16)
    o_ref[0] = jnp.dot(ca, w_ref[0].astype(BF16), preferred_element_type=F32) + b_ref[0]


def _ada_mod(c, w_ada, b_ada):
    depth, d, n6 = w_ada.shape
    b = c.shape[0]
    bp = -(-b // SUBLANES) * SUBLANES
    cp = jnp.zeros((bp, d), F32).at[:b].set(c)
    tn = ADA_COLS
    out = pl.pallas_call(
        _ada_kernel,
        out_shape=jax.ShapeDtypeStruct((depth, bp, n6), F32),
        grid=(depth, n6 // tn),
        in_specs=[pl.BlockSpec((bp, d), lambda l, j: (0, 0)),
                  pl.BlockSpec((1, d, tn), lambda l, j: (l, 0, j)),
                  pl.BlockSpec((1, 1, tn), lambda l, j: (l, 0, j))],
        out_specs=pl.BlockSpec((1, bp, tn), lambda l, j: (l, 0, j)),
        compiler_params=_cparams(("arbitrary", "arbitrary")),
        name="ada_mod",
    )(cp, w_ada, b_ada.reshape(depth, 1, n6))
    return out[:, :b]


def _front_kernel(x_ref, sc_ref, sh_ref, w_ref, cw_ref, cb_ref, cg_ref, cbe_ref, lb_ref,
                  ng_ref, sel_ref, pm_ref, g_ref,
                  yc_ref, yr_ref, ya_ref,
                  shs, h_s, z_s, rf_ref, ri_ref, st_ref, aq_ref, ak_ref, av_ref, kprev, vprev, bias_ref):
    rq_ref, lf_ref, rk_ref, rg_ref = _rec_streams(rf_ref)
    tm = x_ref.shape[1]
    dc = cw_ref.shape[1]
    dr = lb_ref.shape[1]
    da = aq_ref.shape[2]
    rec0 = 2 * dc
    att0 = 2 * dc + 4 * dr
    quarter = tm // 4

    def stage(fn):
        fn()

    def proj(c0, n):
        return jnp.dot(h_s[...], w_ref[:, c0:c0 + n], preferred_element_type=F32)

    def paced_zero(dep):
        bits = lax.bitcast_convert_type(dep[0:SUBLANES, 0:128], jnp.uint32)
        zero = lax.shift_right_logical(lax.shift_right_logical(bits, jnp.uint32(16)), jnp.uint32(16))
        zero = lax.bitcast_convert_type(zero, F32)
        return jnp.tile(zero, (CONV_RS // SUBLANES, dc // 128))

    def conv_rows(lo, hi, dep):
        off = CONV_TAIL - (CONV_WIDTH - 1)
        acc = dep
        for r0 in range(lo, hi, CONV_RS):
            acc = paced_zero(acc)
            for j in range(CONV_WIDTH):
                a, s = divmod(off + j, SUBLANES)
                acc = acc + cw_ref[j:j + 1, :] * shs[s, r0 + a * SUBLANES:r0 + a * SUBLANES + CONV_RS, :]
            y = _layer_norm(acc + cb_ref[...], cg_ref[...], cbe_ref[...])
            yc_ref[0, r0:r0 + CONV_RS, :] = _silu(y).astype(BF16)

    def gate_rows(lo, hi):
        lb = lb_ref[...]
        lb_floor = jnp.maximum(lb, TINY)
        one_m_lb = 1.0 - lb
        for r0 in range(lo, hi, GATE_RS):
            z = z_s[r0:r0 + GATE_RS, :]
            e = jnp.exp(-jnp.abs(z))
            r = 1.0 / (1.0 + e)
            pos = z >= 0.0
            keep = one_m_lb * (jnp.where(pos, 1.0, e) * r)
            lf_ref[0, r0:r0 + GATE_RS, :] = jnp.log(lb_floor + keep)
            rk_ref[0, r0:r0 + GATE_RS, :] = one_m_lb - keep

    @pl.when(pl.program_id(1) == 0)
    def _():
        shs[0, 0:CONV_TAIL, :] = jnp.zeros((CONV_TAIL, dc), F32)

    @pl.when(pl.program_id(1) > 0)
    def _():
        shs[0, 0:CONV_TAIL, :] = shs[0, tm:tm + CONV_TAIL, :]

    h_s[...] = (x_ref[0] * (1.0 + sc_ref[0]) + sh_ref[0]).astype(BF16)

    @stage
    def _():
        pc = proj(0, 2 * dc)
        for r0 in range(0, tm, CONV_RS):
            blk = pc[r0:r0 + CONV_RS, :]
            shs[0, CONV_TAIL + r0:CONV_TAIL + r0 + CONV_RS, :] = (
                blk[:, :dc] * jax.nn.sigmoid(blk[:, dc:]))

    @stage
    def _():
        rq_ref[0] = proj(rec0, dr)
        n_sh = tm + CONV_TAIL - SUBLANES
        for s in range(1, SUBLANES):
            for r0 in range(0, n_sh, CONV_RS):
                n = min(CONV_RS, n_sh - r0)
                shs[s, r0:r0 + n, :] = shs[0, r0 + s:r0 + s + n, :]

    @stage
    def _():
        res = proj(rec0 + dr, dr)
        z_s[...] = res
        conv_rows(0, quarter, res)

    @stage
    def _():
        res = proj(rec0 + 2 * dr, dr)
        ri_ref[0] = res.astype(BF16)
        conv_rows(quarter, 2 * quarter, res)
        gate_rows(0, 2 * quarter)

    @stage
    def _():
        res = proj(rec0 + 3 * dr, dr)
        rg_ref[0] = res
        conv_rows(2 * quarter, 3 * quarter, res)
        gate_rows(2 * quarter, tm)

    @stage
    def _():
        aq_ref[0] = (proj(att0, da) * (LOG2E * ATT_DH ** -0.5)).astype(BF16)
        ak_ref[0] = proj(att0 + da, da).astype(BF16)
        res = proj(att0 + 2 * da, da)
        av_ref[0] = res.astype(BF16)
        conv_rows(3 * quarter, tm, res)

    _rec_body(rf_ref, ri_ref, ng_ref, sel_ref, pm_ref, yr_ref, st_ref)
    _att_body(aq_ref, ak_ref, av_ref, kprev, vprev, g_ref, bias_ref, ya_ref)


def _layer_block(stacked, layer, **kw):
    return pl.BlockSpec((None,) + stacked.shape[1:], lambda i, j: (layer, 0, 0), **kw)


def _front(x, sc, sh, w_all_bf16, layer, conv_w, conv_b, conv_ln_g, conv_ln_b, lb, norm_g, rel_table,
           tm):
    b, t, d = x.shape
    dc = conv_w.shape[1]
    dr = lb.shape[0]
    da = ATT_HEADS * ATT_DH
    row = lambda a: a.reshape(1, -1)
    const = lambda a: pl.BlockSpec(a.shape, lambda i, j: (0,) * a.ndim)
    mod = pl.BlockSpec((1, 1, d), lambda i, j: (i, 0, 0))
    tile = lambda n: pl.BlockSpec((1, tm, n), lambda i, j: (i, j, 0))
    outs = [(dc, BF16), (dr, BF16), (da, BF16)]
    sel, pmask = _rec_tables()
    consts = [conv_w, row(conv_b), row(conv_ln_g), row(conv_ln_b), row(lb), row(norm_g),
              jnp.asarray(sel, BF16), jnp.asarray(pmask, F32), _att_rel_row(rel_table, ATT_QB)]
    return pl.pallas_call(
        _front_kernel,
        out_shape=[jax.ShapeDtypeStruct((b, t, n), dt) for n, dt in outs],
        grid=(b, t // tm),
        in_specs=[tile(d), mod, mod, _layer_block(w_all_bf16, layer)] + [const(a) for a in consts],
        out_specs=[tile(n) for n, _ in outs],
        scratch_shapes=[pltpu.VMEM((SUBLANES, tm + CONV_TAIL, dc), F32),
                        pltpu.VMEM((tm, d), BF16),
                        pltpu.VMEM((tm, dr), F32),
                        pltpu.VMEM((1, tm, 4 * dr), F32),
                        pltpu.VMEM((1, tm, dr), BF16),
                        pltpu.VMEM((dr // REC_D, REC_D, REC_D), F32),
                        pltpu.VMEM((1, tm, da), BF16),
                        pltpu.VMEM((1, tm, da), BF16),
                        pltpu.VMEM((1, tm, da), BF16),
                        pltpu.VMEM((tm, da), BF16),
                        pltpu.VMEM((tm, da), BF16),
                        pltpu.VMEM((ATT_HEADS, ATT_QB, ATT_LEFT * CHUNK + ATT_QB), F32)],
        compiler_params=_cparams(("arbitrary", "arbitrary")),
        name="front",
    )(x, sc, sh, w_all_bf16, *consts)


REC_LEVELS = tuple(CHUNK >> (i + 1) for i in range(int(math.log2(CHUNK))))
REC_SAFE_RANGE = 150.0
REC_SPLIT = 2


def _rec_tables():
    x = np.arange(CHUNK)
    sel = [x[:, None] >= x[None, :]]
    masks = []
    for m in REC_LEVELS:
        bnd = (x // (2 * m)) * (2 * m) + m - 1
        sel.append(bnd[:, None] >= x[None, :])
        same = (x[:, None] // (2 * m)) == (x[None, :] // (2 * m))
        masks.append(same & ((x[:, None] & m) != 0) & ((x[None, :] & m) == 0))
    masks.append(x[:, None] == x[None, :])
    sel = np.concatenate(sel, axis=0).astype(np.float32)
    return np.concatenate([sel] * REC_SPLIT, axis=1), np.stack(masks).astype(np.float32)


def _split_bf16(x):
    parts = []
    for _ in range(REC_SPLIT):
        p = x.astype(BF16)
        parts.append(p)
        x = x - p.astype(F32)
    return parts


def _rec_streams(rf_ref):
    dd = rf_ref.shape[2] // 4
    return [rf_ref.at[:, :, i * dd:(i + 1) * dd] for i in range(4)]


def _rec_body(rf_ref, i_ref, ng_ref, sel_ref, pm_ref, o_ref, st_ref):
    q_ref, lf_ref, k_ref, g_ref = _rec_streams(rf_ref)
    tb = q_ref.shape[1]
    dd = q_ref.shape[2]
    nh = dd // REC_D
    nchunks = tb // CHUNK
    nt = (((1,), (1,)), ((), ()))
    tn = (((0,), (0,)), ((), ()))

    @pl.when(pl.program_id(1) == 0)
    def _():
        st_ref[...] = jnp.zeros(st_ref.shape, F32)

    tot = [jnp.sum(lf_ref[0, c * CHUNK:(c + 1) * CHUNK, :], axis=0, keepdims=True)
           for c in range(nchunks)]
    safe = jnp.min(functools.reduce(jnp.minimum, tot)) >= -REC_SAFE_RANGE

    def prefix(lf, nrows):
        return jnp.dot(sel_ref[0:nrows, :], jnp.concatenate(_split_bf16(lf), axis=0),
                       preferred_element_type=F32)

    def head_out(h, a, qg, kg, vb, eb):
        sl = slice(h * REC_D, (h + 1) * REC_D)
        st = st_ref[h]
        o_h = (lax.dot_general(qg[:, sl], st.astype(BF16), nt, preferred_element_type=F32)
               + jnp.dot(a.astype(BF16), vb[:, sl], preferred_element_type=F32))
        upd = lax.dot_general(vb[:, sl], kg[:, sl], tn, preferred_element_type=F32)
        st_ref[h] = st * eb[:, sl] + upd
        ms = jnp.mean(o_h * o_h, axis=-1, keepdims=True)
        return o_h * lax.rsqrt(ms + LN_EPS) * ng_ref[...]

    def store(rs, outs):
        y = jnp.concatenate(outs, axis=-1) * _silu(g_ref[0, rs, :])
        o_ref[0, rs, :] = y.astype(o_ref.dtype)

    @pl.when(safe)
    def _():
        causal = (lax.broadcasted_iota(jnp.int32, (CHUNK, CHUNK), 0)
                  >= lax.broadcasted_iota(jnp.int32, (CHUNK, CHUNK), 1))
        heads = [slice(h * REC_D, (h + 1) * REC_D) for h in range(nh)]
        rows = [slice(c * CHUNK, (c + 1) * CHUNK) for c in range(nchunks)]
        bcs = [prefix(lf_ref[0, rs, :], CHUNK) for rs in rows]
        qg, kg, vb, eb, att, upd = [], [], [], [], [], []
        for rs, bc in zip(rows, bcs):
            half = 0.5 * bc[CHUNK - 1:CHUNK, :]
            eh = jnp.exp(half)
            qf = q_ref[0, rs, :] * jnp.exp(bc - half)
            kf = k_ref[0, rs, :] * jnp.exp(half - bc)
            qi = qf.astype(BF16)
            ki = kf.astype(BF16)
            qg.append((qf * eh).astype(BF16))
            kg.append((kf * eh).astype(BF16))
            vb.append(i_ref[0, rs, :])
            eb.append(eh * eh)
            att.append([jnp.where(causal, lax.dot_general(qi[:, sl], ki[:, sl], nt,
                                                          preferred_element_type=F32),
                                  0.0).astype(BF16) for sl in heads])
        for c in range(nchunks):
            upd.append([lax.dot_general(vb[c][:, sl], kg[c][:, sl], tn, preferred_element_type=F32)
                        for sl in heads])
        states = []
        for h, sl in enumerate(heads):
            st = st_ref[h]
            per_chunk = []
            for c in range(nchunks):
                per_chunk.append(st.astype(BF16))
                st = st * eb[c][:, sl] + upd[c][h]
            st_ref[h] = st
            states.append(per_chunk)
        for c, rs in enumerate(rows):
            outs = []
            for h, sl in enumerate(heads):
                o_h = (lax.dot_general(qg[c][:, sl], states[h][c], nt, preferred_element_type=F32)
                       + jnp.dot(att[c][h], vb[c][:, sl], preferred_element_type=F32))
                ms = jnp.mean(o_h * o_h, axis=-1, keepdims=True)
                outs.append(o_h * lax.rsqrt(ms + LN_EPS) * ng_ref[...])
            store(rs, outs)

    @pl.when(jnp.logical_not(safe))
    def _():
        row_id = lax.broadcasted_iota(jnp.int32, (CHUNK, dd), 0)

        def chunk_body(ci, carry):
            rs = pl.ds(pl.multiple_of(ci * CHUNK, CHUNK), CHUNK)
            q = q_ref[0, rs, :]
            k = k_ref[0, rs, :]
            pre = prefix(lf_ref[0, rs, :], sel_ref.shape[0])
            bc = pre[0:CHUNK, :]
            bl = bc[CHUNK - 1:CHUNK, :]
            qg = (q * jnp.exp(bc)).astype(BF16)
            kg = (k * jnp.exp(bl - bc)).astype(BF16)
            vb = i_ref[0, rs, :]
            xs = []
            for li, m in enumerate(REC_LEVELS):
                dec = jnp.exp(-jnp.abs(bc - pre[(li + 1) * CHUNK:(li + 2) * CHUNK, :]))
                xs.append((jnp.where((row_id & m) != 0, q, k) * dec).astype(BF16))
            qb16 = q.astype(BF16)
            kb16 = k.astype(BF16)
            outs = []
            for h in range(nh):
                sl = slice(h * REC_D, (h + 1) * REC_D)
                a = (lax.dot_general(qb16[:, sl], kb16[:, sl], nt, preferred_element_type=F32)
                     * pm_ref[len(REC_LEVELS)])
                for li in range(len(REC_LEVELS)):
                    xh = xs[li][:, sl]
                    a = a + lax.dot_general(xh, xh, nt, preferred_element_type=F32) * pm_ref[li]
                outs.append(head_out(h, a, qg, kg, vb, jnp.exp(bl)))
            store(rs, outs)
            return carry

        lax.fori_loop(0, nchunks, chunk_body, 0)


def _att_body(q_ref, k_ref, v_ref, kprev, vprev, g_ref, bias_ref, o_ref):
    tm = q_ref.shape[1]
    da = q_ref.shape[2]
    qb = ATT_QB
    hist = ATT_LEFT * CHUNK
    assert hist <= tm and tm % qb == 0
    t = pl.program_id(1)

    @pl.when((pl.program_id(0) == 0) & (t == 0))
    def _():
        gw = g_ref.shape[2]
        qrow = lax.broadcasted_iota(jnp.int32, (qb, hist + qb), 0)
        kcol = lax.broadcasted_iota(jnp.int32, (qb, hist + qb), 1)
        dchunk = kcol // CHUNK - qrow // CHUNK
        band = (dchunk >= 0) & (dchunk <= ATT_LEFT)
        for h in range(da // ATT_DH):
            gb = jnp.broadcast_to(g_ref[h], (qb, gw))
            skew = pltpu.roll(gb, gw - (qb - 1), 1, stride=1, stride_axis=0)
            bias_ref[h] = jnp.where(band, skew[:, :hist + qb] * LOG2E, NEG_BIG)

    @pl.when(t == 0)
    def _():
        kprev[...] = jnp.zeros(kprev.shape, BF16)
        vprev[...] = jnp.zeros(vprev.shape, BF16)

    nh = da // ATT_DH
    lane_head = lax.broadcasted_iota(jnp.int32, (qb, da), 1) // ATT_DH

    def attend(r0, mask_start):
        q = q_ref[0, r0:r0 + qb, :]
        kk = jnp.concatenate([kprev[tm - (hist - r0):tm, :], k_ref[0, 0:r0 + qb, :]], axis=0)
        vv = jnp.concatenate([vprev[tm - (hist - r0):tm, :], v_ref[0, 0:r0 + qb, :]], axis=0)
        scores = [lax.dot_general(jnp.where(lane_head == h, q, jnp.zeros_like(q)), kk,
                                  (((1,), (1,)), ((), ())), preferred_element_type=F32)
                  for h in range(nh)]
        if mask_start:
            kidx = lax.broadcasted_iota(jnp.int32, (qb, hist + qb), 1)
            in_seq = kidx >= hist - r0 - t * tm
        out = jnp.zeros((qb, da), F32)
        for h in range(nh):
            s = scores[h] + bias_ref[h]
            if mask_start:
                s = jnp.where(in_seq, s, NEG_BIG)
            e = jnp.exp2(s - jnp.max(s, axis=-1, keepdims=True))
            l = jnp.sum(e, axis=-1, keepdims=True)
            oh = jnp.dot(e.astype(BF16), vv, preferred_element_type=F32)
            out = out + jnp.where(lane_head == h, oh / l, 0.0)
        o_ref[0, r0:r0 + qb, :] = out.astype(o_ref.dtype)

    @pl.when(t == 0)
    def _():
        for r0 in range(0, tm, qb):
            attend(r0, True)

    @pl.when(t > 0)
    def _():
        for r0 in range(0, tm, qb):
            attend(r0, False)

    kprev[...] = k_ref[0]
    vprev[...] = v_ref[0]


def _att_rel_row(rel_table, qb):
    hist = ATT_LEFT * CHUNK
    gw = pl.next_power_of_2(2 * qb + hist - 1)
    n_lo = qb - 1 + hist - MAX_REL
    n_hi = gw - n_lo - (2 * MAX_REL + 1)
    nh = rel_table.shape[0]
    tab = rel_table.astype(F32)
    row = jnp.concatenate([jnp.broadcast_to(tab[:, :1], (nh, n_lo)), tab,
                           jnp.broadcast_to(tab[:, -1:], (nh, n_hi))], axis=1)
    return row.reshape(nh, 1, gw)


def _tail_kernel(x_ref, yc_ref, yr_ref, ya_ref, wo_ref, g1_ref, l1g_ref, l1b_ref,
                 sc_ref, sh_ref, g2_ref, w1_ref, w2_ref, l2g_ref, l2b_ref, o_ref, a_s, x1_s, h_s,
                 *, alpha, d_ff, fc):
    tm = x_ref.shape[1]
    groups = [slice(i * tm // TAIL_SPLIT, (i + 1) * tm // TAIL_SPLIT) for i in range(TAIL_SPLIT)]
    ys = [jnp.dot(jnp.concatenate([yc_ref[0, rs, :], yr_ref[0, rs, :], ya_ref[0, rs, :]], axis=-1),
                  wo_ref[...], preferred_element_type=F32) for rs in groups]
    for rs, y in zip(groups, ys):
        x1 = _layer_norm(alpha * x_ref[0, rs, :] + (1.0 + g1_ref[0]) * y, l1g_ref[...], l1b_ref[...])
        x1_s[rs, :] = x1
        h_s[rs, :] = (x1 * (1.0 + sc_ref[0]) + sh_ref[0]).astype(BF16)
    for c in range(d_ff // fc):
        for rs in groups:
            gt = jnp.dot(h_s[rs, :], w1_ref[:, c * fc:(c + 1) * fc], preferred_element_type=F32)
            up = jnp.dot(h_s[rs, :], w1_ref[:, d_ff + c * fc:d_ff + (c + 1) * fc],
                         preferred_element_type=F32)
            a_s[rs, c * fc:(c + 1) * fc] = (_silu(gt) * up).astype(BF16)
    y2s = [jnp.dot(a_s[rs, :], w2_ref[...], preferred_element_type=F32) for rs in groups]
    for rs, y2 in zip(groups, y2s):
        o_ref[0, rs, :] = _layer_norm(alpha * x1_s[rs, :] + (1.0 + g2_ref[0]) * y2,
                                      l2g_ref[...], l2b_ref[...])


def _tail(x, yc, yr, ya, wo_bf16, g1, ln1_g, ln1_b, sc2, sh2, g2, w1_bf16, w2_bf16, ln2_g, ln2_b,
          layer, alpha, tm):
    b, t, d = x.shape
    d_ff = w2_bf16.shape[1]
    tile = lambda a: pl.BlockSpec((1, tm, a.shape[2]), lambda i, j: (i, j, 0))
    mod = pl.BlockSpec((1, 1, d), lambda i, j: (i, 0, 0))
    row = pl.BlockSpec((1, d), lambda i, j: (0, 0))
    weight = lambda a: _layer_block(a, layer, pipeline_mode=pl.Buffered(1))
    return pl.pallas_call(
        functools.partial(_tail_kernel, alpha=alpha, d_ff=d_ff, fc=FFN_FC),
        out_shape=jax.ShapeDtypeStruct((b, t, d), F32),
        grid=(b, t // tm),
        in_specs=[tile(x), tile(yc), tile(yr), tile(ya), weight(wo_bf16), mod, row, row,
                  mod, mod, mod, weight(w1_bf16), weight(w2_bf16), row, row],
        out_specs=tile(x),
        scratch_shapes=[pltpu.VMEM((tm, d_ff), BF16),
                        pltpu.VMEM((tm, d), F32),
                        pltpu.VMEM((tm, d), BF16)],
        compiler_params=_cparams(("arbitrary", "arbitrary")),
        name="tail",
    )(x, yc, yr, ya, wo_bf16, g1, ln1_g.reshape(1, d), ln1_b.reshape(1, d), sc2, sh2, g2,
      w1_bf16, w2_bf16, ln2_g.reshape(1, d), ln2_b.reshape(1, d))


def kernel(x, c, w_ada, b_ada, w_in, conv_w, conv_b, conv_ln_g, conv_ln_b, rec_lower_bound,
           rec_norm_g, rel_bias, w_out, ln1_g, ln1_b, ln2_g, ln2_b, w_ffn_in, w_ffn_out):
    depth = w_in.shape[0]
    b, t, d = x.shape
    alpha = (2 * depth) ** 0.25

    lbs = jax.nn.softmax(rec_lower_bound.astype(F32), axis=0)
    lbs = jnp.cumsum(lbs, axis=0) - lbs[0]
    mods = _ada_mod(c, w_ada, b_ada)
    w_in, w_out, w_ffn_in, w_ffn_out = [w.astype(BF16) for w in (w_in, w_out, w_ffn_in, w_ffn_out)]

    for l in range(depth):
        sh1, sc1, g1, sh2, sc2, g2 = [m.reshape(b, 1, d) for m in jnp.split(mods[l], 6, axis=-1)]
        y_conv, y_rec, y_att = _front(
            x, sc1, sh1, w_in, l, conv_w[l], conv_b[l], conv_ln_g[l], conv_ln_b[l], lbs[l],
            rec_norm_g[l], rel_bias[l], tm=FRONT_ROWS)
        x = _tail(x, y_conv, y_rec, y_att, w_out, g1, ln1_g[l], ln1_b[l], sc2, sh2, g2,
                  w_ffn_in, w_ffn_out, ln2_g[l], ln2_b[l], l, alpha, tm=TILE_ROWS)
    return x
```

```python
import functools
import math

import numpy as np
import jax
import jax.numpy as jnp
from jax import lax
from jax.experimental import pallas as pl
from jax.experimental.pallas import tpu as pltpu

F32 = jnp.float32
BF16 = jnp.bfloat16

CHUNK = 64
CONV_WIDTH = 31
REC_HEADS = 4
REC_D = 128
ATT_HEADS = 4
ATT_DH = 64
ATT_LEFT = 8
MAX_REL = 128
LN_EPS = 1e-5
NEG_BIG = -1e30
TINY = 1e-30
LOG2E = math.log2(math.e)

ADA_COLS = 3072
FRONT_ROWS = 512
TILE_ROWS = 1024
ATT_QB = 256
CONV_TAIL = 32
CONV_RS = 16
GATE_RS = 16
FFN_FC = 256
TAIL_SPLIT = 4
SUBLANES = 8

VMEM_LIMIT = 56 * 1024 * 1024


def _cparams(sem):
    return pltpu.CompilerParams(dimension_semantics=sem, vmem_limit_bytes=VMEM_LIMIT)


def _layer_norm(x, g, b):
    mu = jnp.mean(x, axis=-1, keepdims=True)
    xc = x - mu
    var = jnp.mean(xc * xc, axis=-1, keepdims=True)
    return xc * lax.rsqrt(var + LN_EPS) * g + b


def _silu(x):
    return x * jax.nn.sigmoid(x)


def _ada_kernel(c_ref, w_ref, b_ref, o_ref):
    ca = _silu(c_ref[...]).astype(BF16)
    o_ref[0] = jnp.dot(ca, w_ref[0].astype(BF16), preferred_element_type=F32) + b_ref[0]


def _ada_mod(c, w_ada, b_ada):
    depth, d, n6 = w_ada.shape
    b = c.shape[0]
    bp = -(-b // SUBLANES) * SUBLANES
    cp = jnp.zeros((bp, d), F32).at[:b].set(c)
    tn = ADA_COLS
    out = pl.pallas_call(
        _ada_kernel,
        out_shape=jax.ShapeDtypeStruct((depth, bp, n6), F32),
        grid=(depth, n6 // tn),
        in_specs=[pl.BlockSpec((bp, d), lambda l, j: (0, 0)),
                  pl.BlockSpec((1, d, tn), lambda l, j: (l, 0, j)),
                  pl.BlockSpec((1, 1, tn), lambda l, j: (l, 0, j))],
        out_specs=pl.BlockSpec((1, bp, tn), lambda l, j: (l, 0, j)),
        compiler_params=_cparams(("arbitrary", "arbitrary")),
        name="ada_mod",
    )(cp, w_ada, b_ada.reshape(depth, 1, n6))
    return out[:, :b]


def _front_kernel(x_ref, sc_ref, sh_ref, w_ref, cw_ref, cb_ref, cg_ref, cbe_ref, lb_ref,
                  ng_ref, sel_ref, pm_ref, g_ref,
                  yc_ref, yr_ref, ya_ref,
                  shs, h_s, z_s, rf_ref, ri_ref, st_ref, aq_ref, ak_ref, av_ref, kprev, vprev, bias_ref):
    rq_ref, lf_ref, rk_ref, rg_ref = _rec_streams(rf_ref)
    tm = x_ref.shape[1]
    dc = cw_ref.shape[1]
    dr = lb_ref.shape[1]
    da = aq_ref.shape[2]
    rec0 = 2 * dc
    att0 = 2 * dc + 4 * dr
    quarter = tm // 4

    def stage(fn):
        fn()

    def proj(c0, n):
        return jnp.dot(h_s[...], w_ref[:, c0:c0 + n], preferred_element_type=F32)

    def paced_zero(dep):
        bits = lax.bitcast_convert_type(dep[0:SUBLANES, 0:128], jnp.uint32)
        zero = lax.shift_right_logical(lax.shift_right_logical(bits, jnp.uint32(16)), jnp.uint32(16))
        zero = lax.bitcast_convert_type(zero, F32)
        return jnp.tile(zero, (CONV_RS // SUBLANES, dc // 128))

    def conv_rows(lo, hi, dep):
        off = CONV_TAIL - (CONV_WIDTH - 1)
        acc = dep
        for r0 in range(lo, hi, CONV_RS):
            acc = paced_zero(acc)
            for j in range(CONV_WIDTH):
                a, s = divmod(off + j, SUBLANES)
                acc = acc + cw_ref[j:j + 1, :] * shs[s, r0 + a * SUBLANES:r0 + a * SUBLANES + CONV_RS, :]
            y = _layer_norm(acc + cb_ref[...], cg_ref[...], cbe_ref[...])
            yc_ref[0, r0:r0 + CONV_RS, :] = _silu(y).astype(BF16)

    def gate_rows(lo, hi):
        lb = lb_ref[...]
        lb_floor = jnp.maximum(lb, TINY)
        one_m_lb = 1.0 - lb
        for r0 in range(lo, hi, GATE_RS):
            z = z_s[r0:r0 + GATE_RS, :]
            e = jnp.exp(-jnp.abs(z))
            r = 1.0 / (1.0 + e)
            pos = z >= 0.0
            keep = one_m_lb * (jnp.where(pos, 1.0, e) * r)
            lf_ref[0, r0:r0 + GATE_RS, :] = jnp.log(lb_floor + keep)
            rk_ref[0, r0:r0 + GATE_RS, :] = one_m_lb - keep

    @pl.when(pl.program_id(1) == 0)
    def _():
        shs[0, 0:CONV_TAIL, :] = jnp.zeros((CONV_TAIL, dc), F32)

    @pl.when(pl.program_id(1) > 0)
    def _():
        shs[0, 0:CONV_TAIL, :] = shs[0, tm:tm + CONV_TAIL, :]

    h_s[...] = (x_ref[0] * (1.0 + sc_ref[0]) + sh_ref[0]).astype(BF16)

    @stage
    def _():
        pc = proj(0, 2 * dc)
        for r0 in range(0, tm, CONV_RS):
            blk = pc[r0:r0 + CONV_RS, :]
            shs[0, CONV_TAIL + r0:CONV_TAIL + r0 + CONV_RS, :] = (
                blk[:, :dc] * jax.nn.sigmoid(blk[:, dc:]))

    @stage
    def _():
        rq_ref[0] = proj(rec0, dr)
        n_sh = tm + CONV_TAIL - SUBLANES
        for s in range(1, SUBLANES):
            for r0 in range(0, n_sh, CONV_RS):
                n = min(CONV_RS, n_sh - r0)
                shs[s, r0:r0 + n, :] = shs[0, r0 + s:r0 + s + n, :]

    @stage
    def _():
        res = proj(rec0 + dr, dr)
        z_s[...] = res
        conv_rows(0, quarter, res)

    @stage
    def _():
        res = proj(rec0 + 2 * dr, dr)
        ri_ref[0] = res.astype(BF16)
        conv_rows(quarter, 2 * quarter, res)
        gate_rows(0, 2 * quarter)

    @stage
    def _():
        res = proj(rec0 + 3 * dr, dr)
        rg_ref[0] = res
        conv_rows(2 * quarter, 3 * quarter, res)
        gate_rows(2 * quarter, tm)

    @stage
    def _():
        aq_ref[0] = (proj(att0, da) * (LOG2E * ATT_DH ** -0.5)).astype(BF16)
        ak_ref[0] = proj(att0 + da, da).astype(BF16)
        res = proj(att0 + 2 * da, da)
        av_ref[0] = res.astype(BF16)
        conv_rows(3 * quarter, tm, res)

    _rec_body(rf_ref, ri_ref, ng_ref, sel_ref, pm_ref, yr_ref, st_ref)
    _att_body(aq_ref, ak_ref, av_ref, kprev, vprev, g_ref, bias_ref, ya_ref)


def _layer_block(stacked, layer, **kw):
    return pl.BlockSpec((None,) + stacked.shape[1:], lambda i, j: (layer, 0, 0), **kw)


def _front(x, sc, sh, w_all_bf16, layer, conv_w, conv_b, conv_ln_g, conv_ln_b, lb, norm_g, rel_table,
           tm):
    b, t, d = x.shape
    dc = conv_w.shape[1]
    dr = lb.shape[0]
    da = ATT_HEADS * ATT_DH
    row = lambda a: a.reshape(1, -1)
    const = lambda a: pl.BlockSpec(a.shape, lambda i, j: (0,) * a.ndim)
    mod = pl.BlockSpec((1, 1, d), lambda i, j: (i, 0, 0))
    tile = lambda n: pl.BlockSpec((1, tm, n), lambda i, j: (i, j, 0))
    outs = [(dc, BF16), (dr, BF16), (da, BF16)]
    sel, pmask = _rec_tables()
    consts = [conv_w, row(conv_b), row(conv_ln_g), row(conv_ln_b), row(lb), row(norm_g),
              jnp.asarray(sel, BF16), jnp.asarray(pmask, F32), _att_rel_row(rel_table, ATT_QB)]
    return pl.pallas_call(
        _front_kernel,
        out_shape=[jax.ShapeDtypeStruct((b, t, n), dt) for n, dt in outs],
        grid=(b, t // tm),
        in_specs=[tile(d), mod, mod, _layer_block(w_all_bf16, layer)] + [const(a) for a in consts],
        out_specs=[tile(n) for n, _ in outs],
        scratch_shapes=[pltpu.VMEM((SUBLANES, tm + CONV_TAIL, dc), F32),
                        pltpu.VMEM((tm, d), BF16),
                        pltpu.VMEM((tm, dr), F32),
                        pltpu.VMEM((1, tm, 4 * dr), F32),
                        pltpu.VMEM((1, tm, dr), BF16),
                        pltpu.VMEM((dr // REC_D, REC_D, REC_D), F32),
                        pltpu.VMEM((1, tm, da), BF16),
                        pltpu.VMEM((1, tm, da), BF16),
                        pltpu.VMEM((1, tm, da), BF16),
                        pltpu.VMEM((tm, da), BF16),
                        pltpu.VMEM((tm, da), BF16),
                        pltpu.VMEM((ATT_HEADS, ATT_QB, ATT_LEFT * CHUNK + ATT_QB), F32)],
        compiler_params=_cparams(("arbitrary", "arbitrary")),
        name="front",
    )(x, sc, sh, w_all_bf16, *consts)


REC_LEVELS = tuple(CHUNK >> (i + 1) for i in range(int(math.log2(CHUNK))))
REC_SAFE_RANGE = 150.0
REC_SPLIT = 2


def _rec_tables():
    x = np.arange(CHUNK)
    sel = [x[:, None] >= x[None, :]]
    masks = []
    for m in REC_LEVELS:
        bnd = (x // (2 * m)) * (2 * m) + m - 1
        sel.append(bnd[:, None] >= x[None, :])
        same = (x[:, None] // (2 * m)) == (x[None, :] // (2 * m))
        masks.append(same & ((x[:, None] & m) != 0) & ((x[None, :] & m) == 0))
    masks.append(x[:, None] == x[None, :])
    sel = np.concatenate(sel, axis=0).astype(np.float32)
    return np.concatenate([sel] * REC_SPLIT, axis=1), np.stack(masks).astype(np.float32)


def _split_bf16(x):
    parts = []
    for _ in range(REC_SPLIT):
        p = x.astype(BF16)
        parts.append(p)
        x = x - p.astype(F32)
    return parts


def _rec_streams(rf_ref):
    dd = rf_ref.shape[2] // 4
    return [rf_ref.at[:, :, i * dd:(i + 1) * dd] for i in range(4)]


def _rec_body(rf_ref, i_ref, ng_ref, sel_ref, pm_ref, o_ref, st_ref):
    q_ref, lf_ref, k_ref, g_ref = _rec_streams(rf_ref)
    tb = q_ref.shape[1]
    dd = q_ref.shape[2]
    nh = dd // REC_D
    nchunks = tb // CHUNK
    nt = (((1,), (1,)), ((), ()))
    tn = (((0,), (0,)), ((), ()))

    @pl.when(pl.program_id(1) == 0)
    def _():
        st_ref[...] = jnp.zeros(st_ref.shape, F32)

    tot = [jnp.sum(lf_ref[0, c * CHUNK:(c + 1) * CHUNK, :], axis=0, keepdims=True)
           for c in range(nchunks)]
    safe = jnp.min(functools.reduce(jnp.minimum, tot)) >= -REC_SAFE_RANGE

    def prefix(lf, nrows):
        return jnp.dot(sel_ref[0:nrows, :], jnp.concatenate(_split_bf16(lf), axis=0),
                       preferred_element_type=F32)

    def head_out(h, a, qg, kg, vb, eb):
        sl = slice(h * REC_D, (h + 1) * REC_D)
        st = st_ref[h]
        o_h = (lax.dot_general(qg[:, sl], st.astype(BF16), nt, preferred_element_type=F32)
               + jnp.dot(a.astype(BF16), vb[:, sl], preferred_element_type=F32))
        upd = lax.dot_general(vb[:, sl], kg[:, sl], tn, preferred_element_type=F32)
        st_ref[h] = st * eb[:, sl] + upd
        ms = jnp.mean(o_h * o_h, axis=-1, keepdims=True)
        return o_h * lax.rsqrt(ms + LN_EPS) * ng_ref[...]

    def store(rs, outs):
        y = jnp.concatenate(outs, axis=-1) * _silu(g_ref[0, rs, :])
        o_ref[0, rs, :] = y.astype(o_ref.dtype)

    @pl.when(safe)
    def _():
        causal = (lax.broadcasted_iota(jnp.int32, (CHUNK, CHUNK), 0)
                  >= lax.broadcasted_iota(jnp.int32, (CHUNK, CHUNK), 1))
        heads = [slice(h * REC_D, (h + 1) * REC_D) for h in range(nh)]
        rows = [slice(c * CHUNK, (c + 1) * CHUNK) for c in range(nchunks)]
        bcs = [prefix(lf_ref[0, rs, :], CHUNK) for rs in rows]
        qg, kg, vb, eb, att, upd = [], [], [], [], [], []
        for rs, bc in zip(rows, bcs):
            half = 0.5 * bc[CHUNK - 1:CHUNK, :]
            eh = jnp.exp(half)
            qf = q_ref[0, rs, :] * jnp.exp(bc - half)
            kf = k_ref[0, rs, :] * jnp.exp(half - bc)
            qi = qf.astype(BF16)
            ki = kf.astype(BF16)
            qg.append((qf * eh).astype(BF16))
            kg.append((kf * eh).astype(BF16))
            vb.append(i_ref[0, rs, :])
            eb.append(eh * eh)
            att.append([jnp.where(causal, lax.dot_general(qi[:, sl], ki[:, sl], nt,
                                                          preferred_element_type=F32),
                                  0.0).astype(BF16) for sl in heads])
        for c in range(nchunks):
            upd.append([lax.dot_general(vb[c][:, sl], kg[c][:, sl], tn, preferred_element_type=F32)
                        for sl in heads])
        states = []
        for h, sl in enumerate(heads):
            st = st_ref[h]
            per_chunk = []
            for c in range(nchunks):
                per_chunk.append(st.astype(BF16))
                st = st * eb[c][:, sl] + upd[c][h]
            st_ref[h] = st
            states.append(per_chunk)
        for c, rs in enumerate(rows):
            outs = []
            for h, sl in enumerate(heads):
                o_h = (lax.dot_general(qg[c][:, sl], states[h][c], nt, preferred_element_type=F32)
                       + jnp.dot(att[c][h], vb[c][:, sl], preferred_element_type=F32))
                ms = jnp.mean(o_h * o_h, axis=-1, keepdims=True)
                outs.append(o_h * lax.rsqrt(ms + LN_EPS) * ng_ref[...])
            store(rs, outs)

    @pl.when(jnp.logical_not(safe))
    def _():
        row_id = lax.broadcasted_iota(jnp.int32, (CHUNK, dd), 0)

        def chunk_body(ci, carry):
            rs = pl.ds(pl.multiple_of(ci * CHUNK, CHUNK), CHUNK)
            q = q_ref[0, rs, :]
            k = k_ref[0, rs, :]
            pre = prefix(lf_ref[0, rs, :], sel_ref.shape[0])
            bc = pre[0:CHUNK, :]
            bl = bc[CHUNK - 1:CHUNK, :]
            qg = (q * jnp.exp(bc)).astype(BF16)
            kg = (k * jnp.exp(bl - bc)).astype(BF16)
            vb = i_ref[0, rs, :]
            xs = []
            for li, m in enumerate(REC_LEVELS):
                dec = jnp.exp(-jnp.abs(bc - pre[(li + 1) * CHUNK:(li + 2) * CHUNK, :]))
                xs.append((jnp.where((row_id & m) != 0, q, k) * dec).astype(BF16))
            qb16 = q.astype(BF16)
            kb16 = k.astype(BF16)
            outs = []
            for h in range(nh):
                sl = slice(h * REC_D, (h + 1) * REC_D)
                a = (lax.dot_general(qb16[:, sl], kb16[:, sl], nt, preferred_element_type=F32)
                     * pm_ref[len(REC_LEVELS)])
                for li in range(len(REC_LEVELS)):
                    xh = xs[li][:, sl]
                    a = a + lax.dot_general(xh, xh, nt, preferred_element_type=F32) * pm_ref[li]
                outs.append(head_out(h, a, qg, kg, vb, jnp.exp(bl)))
            store(rs, outs)
            return carry

        lax.fori_loop(0, nchunks, chunk_body, 0)


def _att_body(q_ref, k_ref, v_ref, kprev, vprev, g_ref, bias_ref, o_ref):
    tm = q_ref.shape[1]
    da = q_ref.shape[2]
    qb = ATT_QB
    hist = ATT_LEFT * CHUNK
    assert hist <= tm and tm % qb == 0
    t = pl.program_id(1)

    @pl.when((pl.program_id(0) == 0) & (t == 0))
    def _():
        gw = g_ref.shape[2]
        qrow = lax.broadcasted_iota(jnp.int32, (qb, hist + qb), 0)
        kcol = lax.broadcasted_iota(jnp.int32, (qb, hist + qb), 1)
        dchunk = kcol // CHUNK - qrow // CHUNK
        band = (dchunk >= 0) & (dchunk <= ATT_LEFT)
        for h in range(da // ATT_DH):
            gb = jnp.broadcast_to(g_ref[h], (qb, gw))
            skew = pltpu.roll(gb, gw - (qb - 1), 1, stride=1, stride_axis=0)
            bias_ref[h] = jnp.where(band, skew[:, :hist + qb] * LOG2E, NEG_BIG)

    @pl.when(t == 0)
    def _():
        kprev[...] = jnp.zeros(kprev.shape, BF16)
        vprev[...] = jnp.zeros(vprev.shape, BF16)

    nh = da // ATT_DH
    lane_head = lax.broadcasted_iota(jnp.int32, (qb, da), 1) // ATT_DH

    def attend(r0, mask_start):
        q = q_ref[0, r0:r0 + qb, :]
        kk = jnp.concatenate([kprev[tm - (hist - r0):tm, :], k_ref[0, 0:r0 + qb, :]], axis=0)
        vv = jnp.concatenate([vprev[tm - (hist - r0):tm, :], v_ref[0, 0:r0 + qb, :]], axis=0)
        scores = [lax.dot_general(jnp.where(lane_head == h, q, jnp.zeros_like(q)), kk,
                                  (((1,), (1,)), ((), ())), preferred_element_type=F32)
                  for h in range(nh)]
        if mask_start:
            kidx = lax.broadcasted_iota(jnp.int32, (qb, hist + qb), 1)
            in_seq = kidx >= hist - r0 - t * tm
        out = jnp.zeros((qb, da), F32)
        for h in range(nh):
            s = scores[h] + bias_ref[h]
            if mask_start:
                s = jnp.where(in_seq, s, NEG_BIG)
            e = jnp.exp2(s - jnp.max(s, axis=-1, keepdims=True))
            l = jnp.sum(e, axis=-1, keepdims=True)
            oh = jnp.dot(e.astype(BF16), vv, preferred_element_type=F32)
            out = out + jnp.where(lane_head == h, oh / l, 0.0)
        o_ref[0, r0:r0 + qb, :] = out.astype(o_ref.dtype)

    @pl.when(t == 0)
    def _():
        for r0 in range(0, tm, qb):
            attend(r0, True)

    @pl.when(t > 0)
    def _():
        for r0 in range(0, tm, qb):
            attend(r0, False)

    kprev[...] = k_ref[0]
    vprev[...] = v_ref[0]


def _att_rel_row(rel_table, qb):
    hist = ATT_LEFT * CHUNK
    gw = pl.next_power_of_2(2 * qb + hist - 1)
    n_lo = qb - 1 + hist - MAX_REL
    n_hi = gw - n_lo - (2 * MAX_REL + 1)
    nh = rel_table.shape[0]
    tab = rel_table.astype(F32)
    row = jnp.concatenate([jnp.broadcast_to(tab[:, :1], (nh, n_lo)), tab,
                           jnp.broadcast_to(tab[:, -1:], (nh, n_hi))], axis=1)
    return row.reshape(nh, 1, gw)


def _tail_kernel(x_ref, yc_ref, yr_ref, ya_ref, wo_ref, g1_ref, l1g_ref, l1b_ref,
                 sc_ref, sh_ref, g2_ref, w1_ref, w2_ref, l2g_ref, l2b_ref, o_ref, a_s, x1_s, h_s,
                 *, alpha, d_ff, fc):
    tm = x_ref.shape[1]
    groups = [slice(i * tm // TAIL_SPLIT, (i + 1) * tm // TAIL_SPLIT) for i in range(TAIL_SPLIT)]
    ys = [jnp.dot(jnp.concatenate([yc_ref[0, rs, :], yr_ref[0, rs, :], ya_ref[0, rs, :]], axis=-1),
                  wo_ref[...], preferred_element_type=F32) for rs in groups]
    for rs, y in zip(groups, ys):
        x1 = _layer_norm(alpha * x_ref[0, rs, :] + (1.0 + g1_ref[0]) * y, l1g_ref[...], l1b_ref[...])
        x1_s[rs, :] = x1
        h_s[rs, :] = (x1 * (1.0 + sc_ref[0]) + sh_ref[0]).astype(BF16)
    for c in range(d_ff // fc):
        for rs in groups:
            gt = jnp.dot(h_s[rs, :], w1_ref[:, c * fc:(c + 1) * fc], preferred_element_type=F32)
            up = jnp.dot(h_s[rs, :], w1_ref[:, d_ff + c * fc:d_ff + (c + 1) * fc],
                         preferred_element_type=F32)
            a_s[rs, c * fc:(c + 1) * fc] = (_silu(gt) * up).astype(BF16)
    y2s = [jnp.dot(a_s[rs, :], w2_ref[...], preferred_element_type=F32) for rs in groups]
    for rs, y2 in zip(groups, y2s):
        o_ref[0, rs, :] = _layer_norm(alpha * x1_s[rs, :] + (1.0 + g2_ref[0]) * y2,
                                      l2g_ref[...], l2b_ref[...])


def _tail(x, yc, yr, ya, wo_bf16, g1, ln1_g, ln1_b, sc2, sh2, g2, w1_bf16, w2_bf16, ln2_g, ln2_b,
          layer, alpha, tm):
    b, t, d = x.shape
    d_ff = w2_bf16.shape[1]
    tile = lambda a: pl.BlockSpec((1, tm, a.shape[2]), lambda i, j: (i, j, 0))
    mod = pl.BlockSpec((1, 1, d), lambda i, j: (i, 0, 0))
    row = pl.BlockSpec((1, d), lambda i, j: (0, 0))
    weight = lambda a: _layer_block(a, layer, pipeline_mode=pl.Buffered(1))
    return pl.pallas_call(
        functools.partial(_tail_kernel, alpha=alpha, d_ff=d_ff, fc=FFN_FC),
        out_shape=jax.ShapeDtypeStruct((b, t, d), F32),
        grid=(b, t // tm),
        in_specs=[tile(x), tile(yc), tile(yr), tile(ya), weight(wo_bf16), mod, row, row,
                  mod, mod, mod, weight(w1_bf16), weight(w2_bf16), row, row],
        out_specs=tile(x),
        scratch_shapes=[pltpu.VMEM((tm, d_ff), BF16),
                        pltpu.VMEM((tm, d), F32),
                        pltpu.VMEM((tm, d), BF16)],
        compiler_params=_cparams(("arbitrary", "arbitrary")),
        name="tail",
    )(x, yc, yr, ya, wo_bf16, g1, ln1_g.reshape(1, d), ln1_b.reshape(1, d), sc2, sh2, g2,
      w1_bf16, w2_bf16, ln2_g.reshape(1, d), ln2_b.reshape(1, d))


def kernel(x, c, w_ada, b_ada, w_in, conv_w, conv_b, conv_ln_g, conv_ln_b, rec_lower_bound,
           rec_norm_g, rel_bias, w_out, ln1_g, ln1_b, ln2_g, ln2_b, w_ffn_in, w_ffn_out):
    depth = w_in.shape[0]
    b, t, d = x.shape
    alpha = (2 * depth) ** 0.25

    lbs = jax.nn.softmax(rec_lower_bound.astype(F32), axis=0)
    lbs = jnp.cumsum(lbs, axis=0) - lbs[0]
    mods = _ada_mod(c, w_ada, b_ada)
    w_in, w_out, w_ffn_in, w_ffn_out = [w.astype(BF16) for w in (w_in, w_out, w_ffn_in, w_ffn_out)]

    for l in range(depth):
        sh1, sc1, g1, sh2, sc2, g2 = [m.reshape(b, 1, d) for m in jnp.split(mods[l], 6, axis=-1)]
        y_conv, y_rec, y_att = _front(
            x, sc1, sh1, w_in, l, conv_w[l], conv_b[l], conv_ln_g[l], conv_ln_b[l], lbs[l],
            rec_norm_g[l], rel_bias[l], tm=FRONT_ROWS)
        x = _tail(x, y_conv, y_rec, y_att, w_out, g1, ln1_g[l], ln1_b[l], sc2, sh2, g2,
                  w_ffn_in, w_ffn_out, ln2_g[l], ln2_b[l], l, alpha, tm=TILE_ROWS)
    return x
```
